```python
import math
import jax, jax.numpy as jnp
from jax import lax
import numpy as np

D_MODEL = 1024
BATCH = 8
SEQ = 2048
DEPTH = 4

A_GROUPS = 4
A_GROUP_DIM = 128
A_WIDTH = A_GROUPS * A_GROUP_DIM
A_CHUNK = 128
DN_HEADS = 4
DN_HEAD_DIM = 128
DN_WIDTH = DN_HEADS * DN_HEAD_DIM
DN_CONV = 5
DN_CHUNK = 64
MLA_HEADS = 4
MLA_Q_RANK = 384
MLA_KV_RANK = 256
MLA_NOPE = 128
MLA_ROPE = 64
MLA_V = 128
MLA_WIDTH = MLA_HEADS * MLA_V
ROPE_THETA = 10000.0
Q_BLOCK = 128
MAX_POS_OFFSET = 4096
N_BRANCH = 3
BRANCH_WIDTH = 512
D_FF = 2816
FFN_CONV = 3
RMS_EPS = 1e-6
LN_EPS = 1e-5

IN_SIZES = (2 * A_WIDTH, 3 * DN_WIDTH, DN_WIDTH, 2 * DN_HEADS, 2 * DN_HEADS,
            MLA_Q_RANK, MLA_KV_RANK, MLA_ROPE, N_BRANCH * D_MODEL)
N_IN = 2 * A_WIDTH + 3 * DN_WIDTH + DN_WIDTH + 4 * DN_HEADS + MLA_Q_RANK + MLA_KV_RANK + MLA_ROPE + N_BRANCH * D_MODEL

kernel_name = "hybrid_gmlp_deltanet_mla_encoder"


def _split_sizes(t, sizes):
    idx = [int(i) for i in np.cumsum(sizes)[:-1]]
    return jnp.split(t, idx, axis=-1)


def _rmsnorm(x, w, eps=RMS_EPS):
    xf = x.astype(jnp.float32)
    y = xf * lax.rsqrt(jnp.mean(xf * xf, axis=-1, keepdims=True) + eps)
    return (y * w.astype(jnp.float32)).astype(x.dtype)


def _layernorm(x, w, b, eps=LN_EPS):
    xf = x.astype(jnp.float32)
    mu = jnp.mean(xf, axis=-1, keepdims=True)
    var = jnp.mean(jnp.square(xf - mu), axis=-1, keepdims=True)
    y = (xf - mu) * lax.rsqrt(var + eps)
    return (y * w.astype(jnp.float32) + b.astype(jnp.float32)).astype(x.dtype)


def _l2norm(x, eps=1e-6):
    return x * lax.rsqrt(jnp.sum(x * x, axis=-1, keepdims=True) + eps)


def _dwconv(x, w, pad):
    return lax.conv_general_dilated(
        x, w[:, None, :].astype(x.dtype), window_strides=(1,), padding=[(pad, pad)],
        dimension_numbers=('NWC', 'WIO', 'NWC'), feature_group_count=x.shape[-1])


def _spatial_gating(a_uv, ln_w, ln_b, w_sp, b_sp):
    bn, s, _ = a_uv.shape
    u, v = jnp.split(jax.nn.gelu(a_uv), 2, axis=-1)
    v = _layernorm(v, ln_w, ln_b)
    vc = v.reshape(bn, s // A_CHUNK, A_CHUNK, A_GROUPS, A_GROUP_DIM)
    mixed = jnp.einsum('gpq,bnqgd->bnpgd', w_sp, vc) + b_sp.T[:, :, None]
    return u * mixed.reshape(bn, s, A_WIDTH)


def _gated_delta_chunked(q, k, v, g, beta):
    bn, s, h, dk = q.shape
    dv = v.shape[-1]
    c = DN_CHUNK
    nc = s // c
    to_chunks = lambda t: t.reshape(bn, nc, c, h, -1).transpose(0, 3, 1, 2, 4)
    q, k, v = to_chunks(q), to_chunks(k), to_chunks(v)
    g = g.reshape(bn, nc, c, h).transpose(0, 3, 1, 2)
    beta = beta.reshape(bn, nc, c, h).transpose(0, 3, 1, 2)
    g = jnp.cumsum(g, axis=-1)
    lower_incl = jnp.tril(jnp.ones((c, c), dtype=bool))
    strict = jnp.tril(jnp.ones((c, c), dtype=bool), -1)
    diff = g[..., :, None] - g[..., None, :]
    decay = jnp.where(lower_incl, jnp.exp(jnp.where(lower_incl, diff, 0.0)), 0.0)
    k_beta = k * beta[..., None]
    low = jnp.where(strict, jnp.einsum('bhnid,bhnjd->bhnij', k_beta, k) * decay, 0.0)
    a_mat = low + jnp.eye(c, dtype=low.dtype)
    rhs = jnp.concatenate([v * beta[..., None], k_beta * jnp.exp(g)[..., None]], axis=-1)
    sol = lax.linalg.triangular_solve(a_mat, rhs, left_side=True, lower=True, unit_diagonal=True)
    u_c, w_c = sol[..., :dv], sol[..., dv:]
    attn = jnp.where(lower_incl, jnp.einsum('bhnid,bhnjd->bhnij', q, k) * decay, 0.0)
    q_dec = q * jnp.exp(g)[..., None]
    g_last = g[..., -1]
    k_dec = k * jnp.exp(g_last[..., None] - g)[..., None]
    lead = lambda t: jnp.moveaxis(t, 2, 0)
    xs = (lead(u_c), lead(w_c), lead(attn), lead(q_dec), lead(k_dec), lead(jnp.exp(g_last)))

    def step(state, inp):
        u_i, w_i, attn_i, qd_i, kd_i, gl_i = inp
        v_new = u_i - jnp.einsum('bhcd,bhde->bhce', w_i, state)
        o = jnp.einsum('bhcd,bhde->bhce', qd_i, state) + jnp.einsum('bhij,bhje->bhie', attn_i, v_new)
        state = state * gl_i[..., None, None] + jnp.einsum('bhcd,bhce->bhde', kd_i, v_new)
        return state, o

    state0 = jnp.zeros((bn, h, dk, dv), jnp.float32)
    _, o = lax.scan(step, state0, xs)
    return o.transpose(1, 0, 3, 2, 4).reshape(bn, s, h, dv)


def _bidir_gated_deltanet(qkv, z, beta_raw, alpha_raw, conv_w, a_log, dt_bias, o_norm_w):
    bn, s, _ = qkv.shape
    f32 = jnp.float32
    qkv = jax.nn.silu(_dwconv(qkv, conv_w, DN_CONV // 2))
    q, k, v = jnp.split(qkv, 3, axis=-1)
    heads = lambda t: t.reshape(bn, s, DN_HEADS, DN_HEAD_DIM).astype(f32)
    q = _l2norm(heads(q)) * (DN_HEAD_DIM ** -0.5)
    k = _l2norm(heads(k))
    v = heads(v)
    beta = jax.nn.sigmoid(beta_raw.astype(f32))
    g = -jnp.exp(a_log.astype(f32)).reshape(2 * DN_HEADS) * jax.nn.softplus(
        alpha_raw.astype(f32) + dt_bias.astype(f32).reshape(2 * DN_HEADS))
    flip = lambda t: jnp.flip(t, axis=1)
    o_fwd = _gated_delta_chunked(q, k, v, g[..., :DN_HEADS], beta[..., :DN_HEADS])
    o_bwd = flip(_gated_delta_chunked(flip(q), flip(k), flip(v), flip(g[..., DN_HEADS:]), flip(beta[..., DN_HEADS:])))
    o = _rmsnorm(o_fwd + o_bwd, o_norm_w) * jax.nn.silu(heads(z))
    return o.reshape(bn, s, DN_WIDTH).astype(z.dtype)


def _rope_tables(positions):
    inv_freq = ROPE_THETA ** (-jnp.arange(0, MLA_ROPE, 2, dtype=jnp.float32) / MLA_ROPE)
    ang = positions.astype(jnp.float32)[..., None] * inv_freq
    return jnp.cos(ang)[:, :, None, :], jnp.sin(ang)[:, :, None, :]


def _apply_rope(x, cos, sin):
    xf = x.astype(jnp.float32)
    x1, x2 = jnp.split(xf, 2, axis=-1)
    return jnp.concatenate([x1 * cos - x2 * sin, x2 * cos + x1 * sin], axis=-1).astype(x.dtype)


def _block_attention(q, k, v):
    bn, s, h, dq = q.shape
    dv = v.shape[-1]
    nb = s // Q_BLOCK
    qb = q.reshape(bn, nb, Q_BLOCK, h, dq).transpose(1, 0, 2, 3, 4)
    scale = dq ** -0.5

    def one(q_blk):
        sc = jnp.einsum('bqhd,bkhd->bhqk', q_blk, k, preferred_element_type=jnp.float32) * scale
        p = jax.nn.softmax(sc, axis=-1).astype(v.dtype)
        return jnp.einsum('bhqk,bkhe->bqhe', p, v)

    o = lax.map(one, qb)
    return o.transpose(1, 0, 2, 3, 4).reshape(bn, s, h, dv)


def _mla(cq, ckv, kr, positions, q_norm_w, kv_norm_w, w_uq, w_ukv):
    bn, s, _ = cq.shape
    q = (_rmsnorm(cq, q_norm_w) @ w_uq).reshape(bn, s, MLA_HEADS, MLA_NOPE + MLA_ROPE)
    kv = (_rmsnorm(ckv, kv_norm_w) @ w_ukv).reshape(bn, s, MLA_HEADS, MLA_NOPE + MLA_V)
    q_nope, q_pe = q[..., :MLA_NOPE], q[..., MLA_NOPE:]
    k_nope, v = kv[..., :MLA_NOPE], kv[..., MLA_NOPE:]
    cos, sin = _rope_tables(positions)
    q_pe = _apply_rope(q_pe, cos, sin)
    k_pe = _apply_rope(kr[:, :, None, :], cos, sin)
    q_full = jnp.concatenate([q_nope, q_pe], axis=-1)
    k_full = jnp.concatenate([k_nope, jnp.broadcast_to(k_pe, (bn, s, MLA_HEADS, MLA_ROPE))], axis=-1)
    return _block_attention(q_full, k_full, v).reshape(bn, s, MLA_WIDTH)


def _token_mixer(h, positions, w_in, b_gate, a_ln_w, a_ln_b, a_w_sp, a_b_sp, dn_conv_w, dn_a_log, dn_dt_bias,
                 dn_o_norm_w, mla_q_norm_w, mla_kv_norm_w, mla_w_uq, mla_w_ukv, w_branch, w_o):
    bn, s, _ = h.shape
    proj = h @ w_in
    a_uv, dn_qkv, dn_z, dn_beta, dn_alpha, mla_cq, mla_ckv, mla_kr, gate_raw = _split_sizes(proj, IN_SIZES)
    o_a = _spatial_gating(a_uv, a_ln_w, a_ln_b, a_w_sp, a_b_sp)
    o_b = _bidir_gated_deltanet(dn_qkv, dn_z, dn_beta, dn_alpha, dn_conv_w, dn_a_log, dn_dt_bias, dn_o_norm_w)
    o_c = _mla(mla_cq, mla_ckv, mla_kr, positions, mla_q_norm_w, mla_kv_norm_w, mla_w_uq, mla_w_ukv)
    branches = jnp.stack([o_a, o_b, o_c], axis=2)
    up = jnp.einsum('bsnw,nwd->bsnd', branches, w_branch)
    gates = jax.nn.sigmoid(gate_raw.reshape(bn, s, N_BRANCH, D_MODEL) + b_gate)
    return jnp.sum(gates * up, axis=2) @ w_o


def _conv_ffn(h, w_up, conv_w, conv_b, w_down):
    hu = _dwconv(h @ w_up, conv_w, FFN_CONV // 2) + conv_b
    a, b = jnp.split(hu, 2, axis=-1)
    return (jax.nn.silu(a) * b) @ w_down


def setup_inputs(seed: int = 0) -> dict:
    key = jax.random.key(seed)
    ks = jax.random.split(key, 32)
    f32 = jnp.float32
    nrm = lambda k, shape, scale: jax.random.normal(k, shape, f32) * scale
    x = nrm(ks[0], (BATCH, SEQ, D_MODEL), 1.0)
    c = nrm(ks[1], (BATCH, D_MODEL), 1.0)
    offset = jax.random.randint(ks[2], (BATCH, 1), 0, MAX_POS_OFFSET, dtype=jnp.int32)
    positions = offset + jnp.arange(SEQ, dtype=jnp.int32)[None, :]
    w_ada = nrm(ks[3], (DEPTH, D_MODEL, 6 * D_MODEL), D_MODEL ** -0.5)
    b_ada = nrm(ks[4], (DEPTH, 6 * D_MODEL), 0.01)
    norm_w = 1.0 + nrm(ks[5], (DEPTH, 4, D_MODEL), 0.05)
    w_in = nrm(ks[6], (DEPTH, D_MODEL, N_IN), D_MODEL ** -0.5)
    b_gate = nrm(ks[7], (DEPTH, N_BRANCH, D_MODEL), 0.01)
    a_ln_w = 1.0 + nrm(ks[8], (DEPTH, A_WIDTH), 0.05)
    a_ln_b = nrm(ks[9], (DEPTH, A_WIDTH), 0.01)
    a_w_sp = nrm(ks[10], (DEPTH, A_GROUPS, A_CHUNK, A_CHUNK), A_CHUNK ** -0.5)
    a_b_sp = 1.0 + nrm(ks[11], (DEPTH, A_GROUPS, A_CHUNK), 0.05)
    dn_conv_w = nrm(ks[12], (DEPTH, DN_CONV, 3 * DN_WIDTH), DN_CONV ** -0.5)
    dn_a_log = jnp.log(jax.random.uniform(ks[13], (DEPTH, 2, DN_HEADS), f32, 1.0, 16.0))
    dt = jnp.exp(jax.random.uniform(ks[14], (DEPTH, 2, DN_HEADS), f32, math.log(1e-3), math.log(1e-1)))
    dn_dt_bias = dt + jnp.log(-jnp.expm1(-dt))
    dn_o_norm_w = 1.0 + nrm(ks[15], (DEPTH, DN_HEAD_DIM), 0.05)
    mla_q_norm_w = 1.0 + nrm(ks[16], (DEPTH, MLA_Q_RANK), 0.05)
    mla_kv_norm_w = 1.0 + nrm(ks[17], (DEPTH, MLA_KV_RANK), 0.05)
    mla_w_uq = nrm(ks[18], (DEPTH, MLA_Q_RANK, MLA_HEADS * (MLA_NOPE + MLA_ROPE)), MLA_Q_RANK ** -0.5)
    mla_w_ukv = nrm(ks[19], (DEPTH, MLA_KV_RANK, MLA_HEADS * (MLA_NOPE + MLA_V)), MLA_KV_RANK ** -0.5)
    w_branch = nrm(ks[20], (DEPTH, N_BRANCH, BRANCH_WIDTH, D_MODEL), BRANCH_WIDTH ** -0.5)
    w_o = nrm(ks[21], (DEPTH, D_MODEL, D_MODEL), D_MODEL ** -0.5)
    ffn_w_up = nrm(ks[22], (DEPTH, D_MODEL, 2 * D_FF), D_MODEL ** -0.5)
    ffn_conv_w = nrm(ks[23], (DEPTH, FFN_CONV, 2 * D_FF), FFN_CONV ** -0.5)
    ffn_conv_b = nrm(ks[24], (DEPTH, 2 * D_FF), 0.01)
    ffn_w_down = nrm(ks[25], (DEPTH, D_FF, D_MODEL), D_FF ** -0.5)
    return {"x": x, "c": c, "positions": positions, "w_ada": w_ada, "b_ada": b_ada, "norm_w": norm_w,
            "w_in": w_in, "b_gate": b_gate, "a_ln_w": a_ln_w, "a_ln_b": a_ln_b, "a_w_sp": a_w_sp,
            "a_b_sp": a_b_sp, "dn_conv_w": dn_conv_w, "dn_a_log": dn_a_log, "dn_dt_bias": dn_dt_bias,
            "dn_o_norm_w": dn_o_norm_w, "mla_q_norm_w": mla_q_norm_w, "mla_kv_norm_w": mla_kv_norm_w,
            "mla_w_uq": mla_w_uq, "mla_w_ukv": mla_w_ukv, "w_branch": w_branch, "w_o": w_o,
            "ffn_w_up": ffn_w_up, "ffn_conv_w": ffn_conv_w, "ffn_conv_b": ffn_conv_b, "ffn_w_down": ffn_w_down}


def reference(x, c, positions, w_ada, b_ada, norm_w, w_in, b_gate, a_ln_w, a_ln_b, a_w_sp, a_b_sp, dn_conv_w,
              dn_a_log, dn_dt_bias, dn_o_norm_w, mla_q_norm_w, mla_kv_norm_w, mla_w_uq, mla_w_ukv, w_branch, w_o,
              ffn_w_up, ffn_conv_w, ffn_conv_b, ffn_w_down):
    c_act = jax.nn.silu(c)
    for l in range(DEPTH):
        mod = jnp.split(c_act @ w_ada[l] + b_ada[l], 6, axis=-1)
        sh1, sc1, gt1, sh2, sc2, gt2 = [m[:, None, :] for m in mod]
        h = _rmsnorm(x, norm_w[l, 0]) * (1.0 + sc1) + sh1
        y = _token_mixer(h, positions, w_in[l], b_gate[l], a_ln_w[l], a_ln_b[l], a_w_sp[l], a_b_sp[l],
                         dn_conv_w[l], dn_a_log[l], dn_dt_bias[l], dn_o_norm_w[l], mla_q_norm_w[l],
                         mla_kv_norm_w[l], mla_w_uq[l], mla_w_ukv[l], w_branch[l], w_o[l])
        x = x + gt1 * _rmsnorm(y, norm_w[l, 1])
        h = _rmsnorm(x, norm_w[l, 2]) * (1.0 + sc2) + sh2
        y = _conv_ffn(h, ffn_w_up[l], ffn_conv_w[l], ffn_conv_b[l], ffn_w_down[l])
        x = x + gt2 * _rmsnorm(y, norm_w[l, 3])
    return x
```

```python
import functools
import math

import numpy as np
import jax
import jax.numpy as jnp
from jax import lax
from jax.experimental import pallas as pl
from jax.experimental.pallas import tpu as pltpu

F32 = jnp.float32
BF = jnp.bfloat16

D = 1024
B = 8
S = 2048
DEPTH = 4
T = B * S

A_W = 512
A_CHUNK = 128
DN_H = 4
DN_D = 128
DN_W = 512
DN_CONV = 5
DN_C = 64
DN_NC = S // DN_C
ML_H = 4
ML_QR = 384
ML_KVR = 256
ML_NOPE = 128
ML_ROPE = 64
ML_V = 128
ML_DQ = ML_NOPE + ML_ROPE
ROPE_THETA = 10000.0
D_FF = 2816
RMS_EPS = 1e-6
LN_EPS = 1e-5

C_A = 0
C_QKV = 1024
C_Z = 2560
C_CQ = 3072
C_CKV = 3456
C_KR = 3712
C_BA = 3840
C_GATE = 3968
NP_IN = 7040

VMEM_LIMIT = 56 * 1024 * 1024
HI = lax.Precision.HIGHEST


def _cparams(sem):
    return pltpu.CompilerParams(dimension_semantics=sem, vmem_limit_bytes=VMEM_LIMIT)


def _dot(a, b):
    return jnp.dot(a, b, preferred_element_type=F32)


def _dot_nt(a, b):
    return lax.dot_general(a, b, (((1,), (1,)), ((), ())), preferred_element_type=F32)


def _dot_hi(a, b):
    return jnp.dot(a, b, preferred_element_type=F32, precision=HI)


def _rms(x, w, eps=RMS_EPS):
    return x * lax.rsqrt(jnp.mean(x * x, axis=-1, keepdims=True) + eps) * w


def _sigmoid(x):
    return 1.0 / (1.0 + jnp.exp(-x))


def _silu(x):
    return x * _sigmoid(x)


def _gelu_tanh(x):
    c = math.sqrt(2.0 / math.pi)
    return 0.5 * x * (1.0 + jnp.tanh(c * (x + 0.044715 * (x * x * x))))


def _const_spec(shape):
    n = len(shape)
    return pl.BlockSpec(shape, lambda *_: (0,) * n, pipeline_mode=pl.Buffered(1))


def _mod_kernel(c_ref, w_ref, b_ref, o_ref):
    c = c_ref[...]
    ca = _silu(c).astype(BF)
    o_ref[0] = _dot(ca, w_ref[0].astype(BF)) + b_ref[0]


def _modulation(c, w_ada, b_ada):
    tn = 1536
    return pl.pallas_call(
        _mod_kernel,
        grid=(DEPTH, 6 * D // tn),
        in_specs=[pl.BlockSpec((B, D), lambda l, j: (0, 0)),
                  pl.BlockSpec((1, D, tn), lambda l, j: (l, 0, j)),
                  pl.BlockSpec((1, 1, tn), lambda l, j: (l, 0, j))],
        out_specs=pl.BlockSpec((1, B, tn), lambda l, j: (l, 0, j)),
        out_shape=jax.ShapeDtypeStruct((DEPTH, B, 6 * D), F32),
        compiler_params=_cparams(("parallel", "parallel")),
        name="adaln_mod",
    )(c, w_ada, b_ada.reshape(DEPTH, 1, 6 * D))


def _rope_kernel(pos_ref, invf_ref, sign_ref, cc_ref, ss_ref):
    ang = pos_ref[...].astype(F32) * invf_ref[...]
    c = jnp.cos(ang)
    s = jnp.sin(ang) * sign_ref[...]
    cc_ref[...] = jnp.concatenate([c, c], axis=-1)
    ss_ref[...] = jnp.concatenate([s, s], axis=-1)


def _rope_tables(positions):
    half = ML_ROPE // 2
    inv_freq = ROPE_THETA ** (-jnp.arange(0, ML_ROPE, 2, dtype=F32) / ML_ROPE)
    invf = jnp.tile(inv_freq, 4).reshape(1, 128)
    sign = jnp.tile(jnp.concatenate([-jnp.ones((half,), F32), jnp.ones((half,), F32)]), 2).reshape(1, 128)
    return pl.pallas_call(
        _rope_kernel,
        grid=(B,),
        in_specs=[pl.BlockSpec((S, 1), lambda b: (b, 0)),
                  pl.BlockSpec((1, 128), lambda b: (0, 0)),
                  pl.BlockSpec((1, 128), lambda b: (0, 0))],
        out_specs=[pl.BlockSpec((S, 256), lambda b: (b, 0)),
                   pl.BlockSpec((S, 256), lambda b: (b, 0))],
        out_shape=[jax.ShapeDtypeStruct((T, 256), F32), jax.ShapeDtypeStruct((T, 256), F32)],
        compiler_params=_cparams(("parallel",)),
        name="rope_tables",
    )(positions.reshape(T, 1), invf, sign)


TM_IN = 512


def _in_kernel(x_ref, nw_ref, sc_ref, sh_ref, w_ref, lnw_ref, lnb_ref, wsp_ref, bsp_ref,
               qnw_ref, kvnw_ref, wuq_ref, wukv_ref, cc_ref, ss_ref,
               oa_ref, qkv_ref, z_ref, ba_ref, gate_ref, q_ref, k_ref, v_ref):
    x = x_ref[...]
    h = _rms(x, nw_ref[...]) * (1.0 + sc_ref[0]) + sh_ref[0]
    hb = h.astype(BF)

    def seg(c0, n):
        return _dot(hb, w_ref[:, c0:c0 + n])

    a = _gelu_tanh(seg(C_A, 2 * A_W))
    u = a[:, :A_W]
    v = a[:, A_W:]
    mu = jnp.mean(v, axis=-1, keepdims=True)
    vc = v - mu
    var = jnp.mean(vc * vc, axis=-1, keepdims=True)
    vn = (vc * lax.rsqrt(var + LN_EPS) * lnw_ref[...] + lnb_ref[...]).astype(BF)
    for c in range(TM_IN // A_CHUNK):
        r = slice(c * A_CHUNK, (c + 1) * A_CHUNK)
        for g in range(4):
            l = slice(g * 128, (g + 1) * 128)
            mixed = _dot(wsp_ref[g], vn[r, l]) + bsp_ref[g]
            oa_ref[r, l] = (u[r, l] * mixed).astype(BF)

    qkv_ref[...] = seg(C_QKV, 3 * DN_W).astype(BF)
    z_ref[...] = seg(C_Z, DN_W).astype(BF)
    ba_ref[...] = seg(C_BA, 128)
    gate_ref[...] = seg(C_GATE, 3 * D).astype(BF)

    cc = cc_ref[...]
    ss = ss_ref[...]
    cqn = _rms(seg(C_CQ, ML_QR), qnw_ref[...]).astype(BF)
    qall = _dot(cqn, wuq_ref[...]) * (ML_DQ ** -0.5)
    q_rot = qall[:, 512:768] * cc + qall[:, 768:1024] * ss
    ckvn = _rms(seg(C_CKV, ML_KVR), kvnw_ref[...]).astype(BF)
    kvall = _dot(ckvn, wukv_ref[...])
    krr = seg(C_KR, 128)
    k_pe = (krr[:, :64] * cc[:, :64] + krr[:, 64:] * ss[:, :64]).astype(BF)
    for hh in range(ML_H):
        l = slice(hh * 128, (hh + 1) * 128)
        q_ref[0, hh, :, 0:ML_NOPE] = qall[:, l].astype(BF)
        q_ref[0, hh, :, ML_NOPE:ML_DQ] = q_rot[:, hh * 64:(hh + 1) * 64].astype(BF)
        k_ref[0, hh, :, 0:ML_NOPE] = kvall[:, l].astype(BF)
        k_ref[0, hh, :, ML_NOPE:ML_DQ] = k_pe
        v_ref[0, hh] = kvall[:, 512 + hh * 128:512 + (hh + 1) * 128].astype(BF)


def _in_proj(x, mod3, nw, w_in, lnw, lnb, wsp, bsp, qnw, kvnw, wuq, wukv, cc, ss):
    tm = TM_IN
    per_b = S // tm
    row = lambda n: pl.BlockSpec((tm, n), lambda i: (i, 0))
    modspec = lambda k: pl.BlockSpec((1, 1, D), lambda i: ((i // per_b) * 6 + k, 0, 0))
    hspec = lambda n: pl.BlockSpec((1, ML_H, tm, n), lambda i: (i // per_b, 0, i % per_b, 0))
    return pl.pallas_call(
        _in_kernel,
        grid=(T // tm,),
        in_specs=[row(D), _const_spec((1, D)), modspec(1), modspec(0), _const_spec((D, NP_IN)),
                  _const_spec((1, A_W)), _const_spec((1, A_W)), _const_spec((4, 128, 128)),
                  _const_spec((4, 128, 128)), _const_spec((1, ML_QR)), _const_spec((1, ML_KVR)),
                  _const_spec((ML_QR, 1024)), _const_spec((ML_KVR, 1024)), row(256), row(256)],
        out_specs=[row(A_W), row(3 * DN_W), row(DN_W), row(128), row(3 * D),
                   hspec(ML_DQ), hspec(ML_DQ), hspec(ML_V)],
        out_shape=[jax.ShapeDtypeStruct((T, A_W), BF), jax.ShapeDtypeStruct((T, 3 * DN_W), BF),
                   jax.ShapeDtypeStruct((T, DN_W), BF), jax.ShapeDtypeStruct((T, 128), F32),
                   jax.ShapeDtypeStruct((T, 3 * D), BF),
                   jax.ShapeDtypeStruct((B, ML_H, S, ML_DQ), BF),
                   jax.ShapeDtypeStruct((B, ML_H, S, ML_DQ), BF),
                   jax.ShapeDtypeStruct((B, ML_H, S, ML_V), BF)],
        compiler_params=_cparams(("parallel",)),
        name="in_proj",
    )(x, nw, mod3, mod3, w_in, lnw, lnb, wsp, bsp, qnw, kvnw, wuq, wukv, cc, ss)


TQ = 256


def _attn_kernel(q_ref, k_ref, v_ref, o_ref):
    s = _dot_nt(q_ref[0, 0], k_ref[0, 0])
    m = jnp.max(s, axis=-1, keepdims=True)
    p = jnp.exp(s - m)
    l = jnp.sum(p, axis=-1, keepdims=True)
    o = _dot(p.astype(BF), v_ref[0, 0])
    o_ref[...] = (o / l).astype(BF)


def _attention(q, k, v):
    nq = S // TQ
    return pl.pallas_call(
        _attn_kernel,
        grid=(B, ML_H, nq),
        in_specs=[pl.BlockSpec((1, 1, TQ, ML_DQ), lambda b, h, i: (b, h, i, 0)),
                  pl.BlockSpec((1, 1, S, ML_DQ), lambda b, h, i: (b, h, 0, 0)),
                  pl.BlockSpec((1, 1, S, ML_V), lambda b, h, i: (b, h, 0, 0))],
        out_specs=pl.BlockSpec((TQ, ML_V), lambda b, h, i: (b * nq + i, h)),
        out_shape=jax.ShapeDtypeStruct((T, ML_H * ML_V), BF),
        compiler_params=_cparams(("parallel", "parallel", "parallel")),
        name="mla_attn",
    )(q, k, v)


def _split3(x):
    hi = x.astype(BF)
    r = x - hi.astype(F32)
    mid = r.astype(BF)
    lo = (r - mid.astype(F32)).astype(BF)
    return hi, mid, lo


def _dn_kernel(qkv_ref, z_ref, ba_ref, cw_ref, gpar_ref, onw_ref, o_ref,
               gb_s, qn_s, kn_s, vv_s, u_s, wq_s, kdt_s, at_s, gl_s, os_s):
    c = DN_C
    nc = DN_NC
    ba = ba_ref[...]
    lane = lax.broadcasted_iota(jnp.int32, (1, 128), 1)
    xg = ba + gpar_ref[1:2, :]
    sp = jnp.maximum(xg, 0.0) + jnp.log1p(jnp.exp(-jnp.abs(xg)))
    gb_s[...] = jnp.where(lane < 8, _sigmoid(ba), -jnp.exp(gpar_ref[0:1, :]) * sp)

    ri = lax.broadcasted_iota(jnp.int32, (c, c), 0)
    ci = lax.broadcasted_iota(jnp.int32, (c, c), 1)
    eye = (ri == ci).astype(F32)
    r128 = lax.broadcasted_iota(jnp.int32, (128, 128), 0)
    c128 = lax.broadcasted_iota(jnp.int32, (128, 128), 1)
    eye128 = (r128 == c128).astype(BF)
    incl = (ri >= ci, ri <= ci)
    strict = (ri > ci, ri < ci)
    tri_bf = tuple(m.astype(BF) for m in incl)
    row_s = lax.broadcasted_iota(jnp.int32, (S, 1), 0)

    def conv_silu(col):
        x = qkv_ref[:, col:col + 128].astype(F32)
        cw = cw_ref[:, col:col + 128]
        acc = x * cw[2:3]
        for t in (0, 1, 3, 4):
            off = t - DN_CONV // 2
            xs = pltpu.roll(x, shift=(-off) % S, axis=0)
            valid = jnp.logical_and(row_s + off >= 0, row_s + off < S)
            acc = acc + jnp.where(valid, xs, 0.0) * cw[t:t + 1]
        return _silu(acc)

    for h in range(DN_H):
        qh = conv_silu(h * DN_D)
        kh = conv_silu(DN_W + h * DN_D)
        vv_s[...] = conv_silu(2 * DN_W + h * DN_D)
        qn_s[...] = qh * lax.rsqrt(jnp.sum(qh * qh, axis=-1, keepdims=True) + 1e-6) * (DN_D ** -0.5)
        kn_s[...] = kh * lax.rsqrt(jnp.sum(kh * kh, axis=-1, keepdims=True) + 1e-6)

        def prep(n, carry):
            r0 = pl.multiple_of(n * c, c)
            qc = qn_s[pl.ds(r0, c), :]
            kc = kn_s[pl.ds(r0, c), :]
            vc = vv_s[pl.ds(r0, c), :]
            gbc = gb_s[pl.ds(r0, c), :]
            qcb = qc.astype(BF)
            kcb = kc.astype(BF)
            qk = _dot_nt(qcb, kcb)
            for d in range(2):
                jb = d * DN_H + h
                beta_b = jnp.broadcast_to(gbc[:, jb:jb + 1], (c, 128))
                g_b = jnp.broadcast_to(gbc[:, 8 + jb:9 + jb], (c, 128))
                rhs_g = jnp.concatenate([g_b, jnp.where(strict[d], g_b[:, :c], 0.0)], axis=1)
                cum = sum(_dot(tri_bf[d], part) for part in _split3(rhs_g))
                gc_b = cum[:, :128]
                decay = jnp.where(incl[d], jnp.exp(jnp.where(incl[d], cum[:, 128:], 0.0)), 0.0)
                glast = gc_b[c - 1:c, :] if d == 0 else gc_b[0:1, :]
                e_gc = jnp.exp(gc_b)
                kb = kc * beta_b
                low = jnp.where(strict[d], _dot_nt(kb.astype(BF), kcb) * decay, 0.0)
                inv = eye - jnp.where((ri // 2) == (ci // 2), low, 0.0)
                for bs in (2, 4, 8, 16, 32):
                    e = jnp.where(jnp.logical_and((ri // (2 * bs)) == (ci // (2 * bs)),
                                                  (ri // bs) != (ci // bs)), low, 0.0)
                    inv = inv - _dot_hi(_dot_hi(inv, e), inv)
                rhs = jnp.concatenate([vc * beta_b, kb * e_gc], axis=1)
                sol = _dot_hi(inv, rhs)
                u_s[d, pl.ds(r0, c), :] = sol[:, :128]
                wq_s[d, n, 0:c, :] = sol[:, 128:].astype(BF)
                wq_s[d, n, c:2 * c, :] = (qc * e_gc).astype(BF)
                k_dec = (kc * jnp.exp(glast - gc_b)).astype(BF)
                kdt_s[d, n] = _dot_nt(eye128, k_dec).astype(BF)
                at_s[d, n] = jnp.where(incl[d], qk * decay, 0.0).astype(BF)
                gl_s[d, n] = jnp.broadcast_to(jnp.exp(glast), (8, 128))
            return carry

        lax.fori_loop(0, nc, prep, 0)

        def scan(i, states):
            new_states = []
            for d in range(2):
                n = i if d == 0 else nc - 1 - i
                r0 = pl.multiple_of(n * c, c)
                st = states[d]
                pr = _dot(wq_s[d, n], st.astype(BF))
                v_new = (u_s[d, pl.ds(r0, c), :] - pr[:c]).astype(BF)
                os_s[d, pl.ds(r0, c), :] = pr[c:] + _dot(at_s[d, n], v_new)
                new_states.append(st * gl_s[d, n][0:1, :] + _dot(kdt_s[d, n], v_new))
            return tuple(new_states)

        zero = jnp.zeros((DN_D, DN_D), F32)
        lax.fori_loop(0, nc, scan, (zero, zero))

        o = os_s[0] + os_s[1]
        zz = z_ref[:, h * DN_D:(h + 1) * DN_D].astype(F32)
        o_ref[:, h * DN_D:(h + 1) * DN_D] = (_rms(o, onw_ref[...]) * _silu(zz)).astype(BF)


def _deltanet(qkv, z, ba, conv_w, gpar, onw):
    c, nc = DN_C, DN_NC
    return pl.pallas_call(
        _dn_kernel,
        grid=(B,),
        in_specs=[pl.BlockSpec((S, 3 * DN_W), lambda b: (b, 0)),
                  pl.BlockSpec((S, DN_W), lambda b: (b, 0)),
                  pl.BlockSpec((S, 128), lambda b: (b, 0)),
                  _const_spec((DN_CONV, 3 * DN_W)), _const_spec((2, 128)), _const_spec((1, DN_D))],
        out_specs=pl.BlockSpec((S, DN_W), lambda b: (b, 0)),
        out_shape=jax.ShapeDtypeStruct((T, DN_W), BF),
        scratch_shapes=[pltpu.VMEM((S, 128), F32), pltpu.VMEM((S, 128), F32), pltpu.VMEM((S, 128), F32),
                        pltpu.VMEM((S, 128), F32), pltpu.VMEM((2, S, 128), F32),
                        pltpu.VMEM((2, nc, 2 * c, 128), BF), pltpu.VMEM((2, nc, 128, c), BF),
                        pltpu.VMEM((2, nc, c, c), BF), pltpu.VMEM((2, nc, 8, 128), F32),
                        pltpu.VMEM((2, S, 128), F32)],
        compiler_params=_cparams(("parallel",)),
        name="deltanet",
    )(qkv, z, ba, conv_w, gpar, onw)


TM_MG = 512


def _merge_kernel(oa_ref, ob_ref, oc_ref, gate_ref, bg_ref, wbr_ref, wo_ref, x_ref, nw1_ref, gt1_ref,
                  nw2_ref, sc2_ref, sh2_ref, x1_ref, h2_ref):
    acc = None
    for n, o_ref in enumerate((oa_ref, ob_ref, oc_ref)):
        up = _dot(o_ref[...], wbr_ref[n])
        g = _sigmoid(gate_ref[:, n * D:(n + 1) * D].astype(F32) + bg_ref[n:n + 1, :])
        acc = g * up if acc is None else acc + g * up
    y = _dot(acc.astype(BF), wo_ref[...])
    x1 = x_ref[...] + gt1_ref[0] * _rms(y, nw1_ref[...])
    x1_ref[...] = x1
    h2_ref[...] = (_rms(x1, nw2_ref[...]) * (1.0 + sc2_ref[0]) + sh2_ref[0]).astype(BF)


def _merge(oa, ob, oc, gate, bg, wbr, wo, x, mod3, nw1, nw2):
    tm = TM_MG
    per_b = S // tm
    row = lambda n: pl.BlockSpec((tm, n), lambda i: (i, 0))
    modspec = lambda k: pl.BlockSpec((1, 1, D), lambda i: ((i // per_b) * 6 + k, 0, 0))
    return pl.pallas_call(
        _merge_kernel,
        grid=(T // tm,),
        in_specs=[row(512), row(512), row(512), row(3 * D), _const_spec((3, D)),
                  _const_spec((3, 512, D)), _const_spec((D, D)), row(D), _const_spec((1, D)), modspec(2),
                  _const_spec((1, D)), modspec(4), modspec(3)],
        out_specs=[row(D), row(D)],
        out_shape=[jax.ShapeDtypeStruct((T, D), F32), jax.ShapeDtypeStruct((T, D), BF)],
        compiler_params=_cparams(("parallel",)),
        name="merge",
    )(oa, ob, oc, gate, bg, wbr, wo, x, nw1, mod3, nw2, mod3, mod3)


TF = 256


def _ffn_kernel(h_ref, wa_ref, wb_ref, cwa_ref, cwb_ref, cba_ref, cbb_ref, wd_ref, y_ref):
    j = pl.program_id(1)
    hb = h_ref[...]
    row = lax.broadcasted_iota(jnp.int32, (S, 1), 0)

    def up_conv(w_ref, cw_ref, cb_ref):
        p = _dot(hb, w_ref[...])
        prev = jnp.where(row >= 1, pltpu.roll(p, shift=1, axis=0), 0.0)
        nxt = jnp.where(row < S - 1, pltpu.roll(p, shift=S - 1, axis=0), 0.0)
        return prev * cw_ref[0:1, :] + p * cw_ref[1:2, :] + nxt * cw_ref[2:3, :] + cb_ref[...]

    a = up_conv(wa_ref, cwa_ref, cba_ref)
    b = up_conv(wb_ref, cwb_ref, cbb_ref)
    act = (_silu(a) * b).astype(BF)
    contrib = _dot(act, wd_ref[...])

    @pl.when(j == 0)
    def _():
        y_ref[...] = contrib

    @pl.when(j > 0)
    def _():
        y_ref[...] += contrib


def _ffn(h2, w_up, conv_w, conv_b, w_down):
    nf = D_FF // TF
    return pl.pallas_call(
        _ffn_kernel,
        grid=(B, nf),
        in_specs=[pl.BlockSpec((S, D), lambda b, j: (b, 0)),
                  pl.BlockSpec((D, TF), lambda b, j: (0, j)),
                  pl.BlockSpec((D, TF), lambda b, j: (0, nf + j)),
                  pl.BlockSpec((3, TF), lambda b, j: (0, j)),
                  pl.BlockSpec((3, TF), lambda b, j: (0, nf + j)),
                  pl.BlockSpec((1, TF), lambda b, j: (0, j)),
                  pl.BlockSpec((1, TF), lambda b, j: (0, nf + j)),
                  pl.BlockSpec((TF, D), lambda b, j: (j, 0))],
        out_specs=pl.BlockSpec((S, D), lambda b, j: (b, 0)),
        out_shape=jax.ShapeDtypeStruct((T, D), F32),
        compiler_params=_cparams(("parallel", "arbitrary")),
        name="conv_ffn",
    )(h2, w_up, w_up, conv_w, conv_w, conv_b, conv_b, w_down)


TM_RS = 1024


def _resid_kernel(x_ref, y_ref, nw_ref, gt_ref, o_ref):
    o_ref[...] = x_ref[...] + gt_ref[0] * _rms(y_ref[...], nw_ref[...])


def _resid(x1, y, mod3, nw3):
    tm = TM_RS
    per_b = S // tm
    row = lambda n: pl.BlockSpec((tm, n), lambda i: (i, 0))
    return pl.pallas_call(
        _resid_kernel,
        grid=(T // tm,),
        in_specs=[row(D), row(D), _const_spec((1, D)),
                  pl.BlockSpec((1, 1, D), lambda i: ((i // per_b) * 6 + 5, 0, 0))],
        out_specs=row(D),
        out_shape=jax.ShapeDtypeStruct((T, D), F32),
        compiler_params=_cparams(("parallel",)),
        name="ffn_resid",
    )(x1, y, nw3, mod3)


def _prep_w_in(w_in):
    a_uv, qkv, z, beta, alpha, cq, ckv, kr, gate = jnp.split(
        w_in, [1024, 2560, 3072, 3080, 3088, 3472, 3728, 3792], axis=-1)
    kr_sw = jnp.concatenate([kr[..., 32:], kr[..., :32]], axis=-1)
    pad = jnp.zeros(w_in.shape[:-1] + (112,), w_in.dtype)
    return jnp.concatenate([a_uv, qkv, z, cq, ckv, kr, kr_sw, beta, alpha, pad, gate], axis=-1).astype(BF)


def _prep_w_uq(w_uq):
    wq = w_uq.reshape(DEPTH, ML_QR, ML_H, ML_DQ)
    nope = wq[..., :ML_NOPE].reshape(DEPTH, ML_QR, ML_H * ML_NOPE)
    pe = wq[..., ML_NOPE:]
    pe_sw = jnp.concatenate([pe[..., 32:], pe[..., :32]], axis=-1)
    return jnp.concatenate([nope, pe.reshape(DEPTH, ML_QR, 256), pe_sw.reshape(DEPTH, ML_QR, 256)],
                           axis=-1).astype(BF)


def _prep_w_ukv(w_ukv):
    wk = w_ukv.reshape(DEPTH, ML_KVR, ML_H, ML_NOPE + ML_V)
    return jnp.concatenate([wk[..., :ML_NOPE].reshape(DEPTH, ML_KVR, 512),
                            wk[..., ML_NOPE:].reshape(DEPTH, ML_KVR, 512)], axis=-1).astype(BF)


@jax.jit
def _forward(x, c, positions, w_ada, b_ada, norm_w, w_in, b_gate, a_ln_w, a_ln_b, a_w_sp, a_b_sp, dn_conv_w,
             dn_a_log, dn_dt_bias, dn_o_norm_w, mla_q_norm_w, mla_kv_norm_w, mla_w_uq, mla_w_ukv, w_branch,
             w_o, ffn_w_up, ffn_conv_w, ffn_conv_b, ffn_w_down):
    mod = _modulation(c, w_ada, b_ada)
    cc, ss = _rope_tables(positions)
    w_in_r = _prep_w_in(w_in)
    wuq_r = _prep_w_uq(mla_w_uq)
    wukv_r = _prep_w_ukv(mla_w_ukv)
    wsp = a_w_sp.astype(BF)
    bsp = jnp.broadcast_to(a_b_sp[..., None], (DEPTH, 4, 128, 128))
    wbr = w_branch.astype(BF)
    wo = w_o.astype(BF)
    wup = ffn_w_up.astype(BF)
    wdn = ffn_w_down.astype(BF)
    zeros8 = jnp.zeros((DEPTH, 8), F32)
    gpar = jnp.stack([
        jnp.concatenate([zeros8, dn_a_log.reshape(DEPTH, 8), jnp.zeros((DEPTH, 112), F32)], axis=-1),
        jnp.concatenate([zeros8, dn_dt_bias.reshape(DEPTH, 8), jnp.zeros((DEPTH, 112), F32)], axis=-1)],
        axis=1)

    xf = x.reshape(T, D)
    for l in range(DEPTH):
        mod3 = mod[l].reshape(B * 6, 1, D)
        nw = norm_w[l]
        oa, qkv, z, ba, gate, q, k, v = _in_proj(
            xf, mod3, nw[0:1], w_in_r[l], a_ln_w[l].reshape(1, A_W), a_ln_b[l].reshape(1, A_W), wsp[l], bsp[l],
            mla_q_norm_w[l].reshape(1, ML_QR), mla_kv_norm_w[l].reshape(1, ML_KVR), wuq_r[l], wukv_r[l], cc, ss)
        ob = _deltanet(qkv, z, ba, dn_conv_w[l], gpar[l], dn_o_norm_w[l].reshape(1, DN_D))
        oc = _attention(q, k, v)
        x1, h2 = _merge(oa, ob, oc, gate, b_gate[l], wbr[l], wo[l], xf, mod3, nw[1:2], nw[2:3])
        y = _ffn(h2, wup[l], ffn_conv_w[l], ffn_conv_b[l].reshape(1, 2 * D_FF), wdn[l])
        xf = _resid(x1, y, mod3, nw[3:4])
    return xf.reshape(B, S, D)


def kernel(x, c, positions, w_ada, b_ada, norm_w, w_in, b_gate, a_ln_w, a_ln_b, a_w_sp, a_b_sp, dn_conv_w,
           dn_a_log, dn_dt_bias, dn_o_norm_w, mla_q_norm_w, mla_kv_norm_w, mla_w_uq, mla_w_ukv, w_branch, w_o,
           ffn_w_up, ffn_conv_w, ffn_conv_b, ffn_w_down):
    return _forward(x, c, positions, w_ada, b_ada, norm_w, w_in, b_gate, a_ln_w, a_ln_b, a_w_sp, a_b_sp,
                    dn_conv_w, dn_a_log, dn_dt_bias, dn_o_norm_w, mla_q_norm_w, mla_kv_norm_w, mla_w_uq,
                    mla_w_ukv, w_branch, w_o, ffn_w_up, ffn_conv_w, ffn_conv_b, ffn_w_down)
```

```python
import math

import jax
import jax.numpy as jnp
from jax import lax
from jax.experimental import pallas as pl
from jax.experimental.pallas import tpu as pltpu

F32 = jnp.float32
BF = jnp.bfloat16

D = 1024
B = 8
S = 2048
DEPTH = 4
T = B * S

A_W = 512
A_CHUNK = 128
DN_H = 4
DN_D = 128
DN_W = 512
DN_CONV = 5
DN_C = 128
DN_NC = S // DN_C
DN_HP = 2
DN_UNROLL = 2
ML_H = 4
ML_QR = 384
ML_KVR = 256
ML_NOPE = 128
ML_ROPE = 64
ML_V = 128
ML_DQ = ML_NOPE + ML_ROPE
ROPE_THETA = 10000.0
D_FF = 2816
RMS_EPS = 1e-6
LN_EPS = 1e-5

C_A = 0
C_QKV = 1024
C_Z = 2560
C_CQ = 3072
C_CKV = 3456
C_KR = 3712
C_BA = 3840
C_GATE = 4096
NP_IN = 7168

VMEM_LIMIT = 56 * 1024 * 1024


def _cparams(sem):
    return pltpu.CompilerParams(dimension_semantics=sem, vmem_limit_bytes=VMEM_LIMIT)


def _dot(a, b):
    return jnp.dot(a, b, preferred_element_type=F32)


def _dot_nt(a, b):
    return lax.dot_general(a, b, (((1,), (1,)), ((), ())), preferred_element_type=F32)


def _rms(x, w, eps=RMS_EPS):
    return x * lax.rsqrt(jnp.mean(x * x, axis=-1, keepdims=True) + eps) * w


def _sigmoid(x):
    return 1.0 / (1.0 + jnp.exp(-x))


def _silu(x):
    return x * _sigmoid(x)


def _gelu_tanh(x):
    c = math.sqrt(2.0 / math.pi)
    return 0.5 * x * (1.0 + jnp.tanh(c * (x + 0.044715 * (x * x * x))))


def _layer_spec(l, tail, idx=None):
    idx = (0,) * len(tail) if idx is None else idx
    return pl.BlockSpec((1,) + tuple(tail), lambda *_: (l,) + tuple(idx), pipeline_mode=pl.Buffered(1))


def _norm_spec(l, k):
    return pl.BlockSpec((1, 1, D), lambda *_: (l * 4 + k, 0, 0), pipeline_mode=pl.Buffered(1))


def _mod_kernel(c_ref, w_ref, b_ref, o_ref):
    c = c_ref[...]
    ca = _silu(c).astype(BF)
    o_ref[0] = _dot(ca, w_ref[0].astype(BF)) + b_ref[0]


def _modulation(c, w_ada, b_ada):
    tn = 1536
    return pl.pallas_call(
        _mod_kernel,
        grid=(DEPTH, 6 * D // tn),
        in_specs=[pl.BlockSpec((B, D), lambda l, j: (0, 0)),
                  pl.BlockSpec((1, D, tn), lambda l, j: (l, 0, j)),
                  pl.BlockSpec((1, 1, tn), lambda l, j: (l, 0, j))],
        out_specs=pl.BlockSpec((1, B, tn), lambda l, j: (l, 0, j)),
        out_shape=jax.ShapeDtypeStruct((DEPTH, B, 6 * D), F32),
        compiler_params=_cparams(("parallel", "parallel")),
        name="adaln_mod",
    )(c, w_ada, b_ada.reshape(DEPTH, 1, 6 * D))


def _rope_kernel(pos_ref, invf_ref, sign_ref, cc_ref, ss_ref):
    ang = pos_ref[...].astype(F32) * invf_ref[...]
    c = jnp.cos(ang)
    s = jnp.sin(ang) * sign_ref[...]
    cc_ref[...] = jnp.concatenate([c, c], axis=-1)
    ss_ref[...] = jnp.concatenate([s, s], axis=-1)


def _rope_tables(positions):
    half = ML_ROPE // 2
    inv_freq = ROPE_THETA ** (-jnp.arange(0, ML_ROPE, 2, dtype=F32) / ML_ROPE)
    invf = jnp.tile(inv_freq, 4).reshape(1, 128)
    sign = jnp.tile(jnp.concatenate([-jnp.ones((half,), F32), jnp.ones((half,), F32)]), 2).reshape(1, 128)
    return pl.pallas_call(
        _rope_kernel,
        grid=(B,),
        in_specs=[pl.BlockSpec((S, 1), lambda b: (b, 0)),
                  pl.BlockSpec((1, 128), lambda b: (0, 0)),
                  pl.BlockSpec((1, 128), lambda b: (0, 0))],
        out_specs=[pl.BlockSpec((S, 256), lambda b: (b, 0)),
                   pl.BlockSpec((S, 256), lambda b: (b, 0))],
        out_shape=[jax.ShapeDtypeStruct((T, 256), F32), jax.ShapeDtypeStruct((T, 256), F32)],
        compiler_params=_cparams(("parallel",)),
        name="rope_tables",
    )(positions.reshape(T, 1), invf, sign)


TM_IN = 512


def _in_kernel(x_ref, nw_ref, sc_ref, sh_ref, w_ref, lnw_ref, lnb_ref, wsp_ref, bsp_ref,
               qnw_ref, kvnw_ref, wuq_ref, wukv_ref, cc_ref, ss_ref,
               oa_ref, qkv_ref, z_ref, ba_ref, gate_ref, q_ref, k_ref, v_ref):
    x = x_ref[...]
    h = _rms(x, nw_ref[0]) * (1.0 + sc_ref[0]) + sh_ref[0]
    hb = h.astype(BF)

    def seg(c0, n):
        return _dot(hb, w_ref[0, :, c0:c0 + n])

    a = _gelu_tanh(seg(C_A, 2 * A_W))
    u = a[:, :A_W]
    v = a[:, A_W:]
    mu = jnp.mean(v, axis=-1, keepdims=True)
    vc = v - mu
    var = jnp.mean(vc * vc, axis=-1, keepdims=True)
    vn = (vc * lax.rsqrt(var + LN_EPS) * lnw_ref[0] + lnb_ref[0]).astype(BF)
    for c in range(TM_IN // A_CHUNK):
        r = slice(c * A_CHUNK, (c + 1) * A_CHUNK)
        for g in range(4):
            l = slice(g * 128, (g + 1) * 128)
            mixed = _dot(wsp_ref[0, g], vn[r, l]) + bsp_ref[0, g]
            oa_ref[r, l] = (u[r, l] * mixed).astype(BF)

    qkv_ref[...] = seg(C_QKV, 3 * DN_W).astype(BF)
    z_ref[...] = seg(C_Z, DN_W).astype(BF)
    ba_ref[...] = seg(C_BA, 256)
    gate_ref[...] = seg(C_GATE, 3 * D).astype(BF)

    cc = cc_ref[...]
    ss = ss_ref[...]
    cqn = _rms(seg(C_CQ, ML_QR), qnw_ref[0]).astype(BF)
    qall = _dot(cqn, wuq_ref[0]) * (ML_DQ ** -0.5)
    q_rot = qall[:, 512:768] * cc + qall[:, 768:1024] * ss
    ckvn = _rms(seg(C_CKV, ML_KVR), kvnw_ref[0]).astype(BF)
    kvall = _dot(ckvn, wukv_ref[0])
    krr = seg(C_KR, 128)
    k_pe = (krr[:, :64] * cc[:, :64] + krr[:, 64:] * ss[:, :64]).astype(BF)
    for hh in range(ML_H):
        l = slice(hh * 128, (hh + 1) * 128)
        q_ref[0, hh, :, 0:ML_NOPE] = qall[:, l].astype(BF)
        q_ref[0, hh, :, ML_NOPE:ML_DQ] = q_rot[:, hh * 64:(hh + 1) * 64].astype(BF)
        k_ref[0, hh, :, 0:ML_NOPE] = kvall[:, l].astype(BF)
        k_ref[0, hh, :, ML_NOPE:ML_DQ] = k_pe
        v_ref[0, hh] = kvall[:, 512 + hh * 128:512 + (hh + 1) * 128].astype(BF)


def _in_proj(l, x, mod3, nw, w_in, lnw, lnb, wsp, bsp, qnw, kvnw, wuq, wukv, cc, ss):
    tm = TM_IN
    per_b = S // tm
    row = lambda n: pl.BlockSpec((tm, n), lambda i: (i, 0))
    modspec = lambda k: pl.BlockSpec((1, 1, D), lambda i: (l * B * 6 + (i // per_b) * 6 + k, 0, 0))
    hspec = lambda n: pl.BlockSpec((1, ML_H, tm, n), lambda i: (i // per_b, 0, i % per_b, 0))
    return pl.pallas_call(
        _in_kernel,
        grid=(T // tm,),
        in_specs=[row(D), _norm_spec(l, 0), modspec(1), modspec(0), _layer_spec(l, (D, NP_IN)),
                  _layer_spec(l, (1, A_W)), _layer_spec(l, (1, A_W)), _layer_spec(l, (4, 128, 128)),
                  _layer_spec(l, (4, 128, 128)), _layer_spec(l, (1, ML_QR)), _layer_spec(l, (1, ML_KVR)),
                  _layer_spec(l, (ML_QR, 1024)), _layer_spec(l, (ML_KVR, 1024)), row(256), row(256)],
        out_specs=[row(A_W), row(3 * DN_W), row(DN_W), row(256), row(3 * D),
                   hspec(ML_DQ), hspec(ML_DQ), hspec(ML_V)],
        out_shape=[jax.ShapeDtypeStruct((T, A_W), BF), jax.ShapeDtypeStruct((T, 3 * DN_W), BF),
                   jax.ShapeDtypeStruct((T, DN_W), BF), jax.ShapeDtypeStruct((T, 256), F32),
                   jax.ShapeDtypeStruct((T, 3 * D), BF),
                   jax.ShapeDtypeStruct((B, ML_H, S, ML_DQ), BF),
                   jax.ShapeDtypeStruct((B, ML_H, S, ML_DQ), BF),
                   jax.ShapeDtypeStruct((B, ML_H, S, ML_V), BF)],
        compiler_params=_cparams(("parallel",)),
        name="in_proj",
    )(x, nw, mod3, mod3, w_in, lnw, lnb, wsp, bsp, qnw, kvnw, wuq, wukv, cc, ss)


TQ = 1024
TQ_SUB = 128


def _attn_kernel(q_ref, k_ref, v_ref, o_ref):
    k = k_ref[0, 0]
    v = v_ref[0, 0]
    nsub = TQ // TQ_SUB

    def scores(i):
        return _dot_nt(q_ref[0, 0, i * TQ_SUB:(i + 1) * TQ_SUB, :], k)

    s_next = scores(0)
    for i in range(nsub):
        s = s_next
        if i + 1 < nsub:
            s_next = scores(i + 1)
        m = jnp.max(s, axis=-1, keepdims=True)
        p = jnp.exp(s - m)
        l = jnp.sum(p, axis=-1, keepdims=True)
        o = _dot(p.astype(BF), v)
        o_ref[i * TQ_SUB:(i + 1) * TQ_SUB, :] = (o / l).astype(BF)


def _attention(q, k, v):
    nq = S // TQ
    return pl.pallas_call(
        _attn_kernel,
        grid=(B, ML_H, nq),
        in_specs=[pl.BlockSpec((1, 1, TQ, ML_DQ), lambda b, h, i: (b, h, i, 0)),
                  pl.BlockSpec((1, 1, S, ML_DQ), lambda b, h, i: (b, h, 0, 0)),
                  pl.BlockSpec((1, 1, S, ML_V), lambda b, h, i: (b, h, 0, 0))],
        out_specs=pl.BlockSpec((TQ, ML_V), lambda b, h, i: (b * nq + i, h)),
        out_shape=jax.ShapeDtypeStruct((T, ML_H * ML_V), BF),
        compiler_params=_cparams(("parallel", "parallel", "parallel")),
        name="mla_attn",
    )(q, k, v)


def _split3(x):
    hi = x.astype(BF)
    r = x - hi.astype(F32)
    mid = r.astype(BF)
    lo = (r - mid.astype(F32)).astype(BF)
    return hi, mid, lo


def _dn_kernel(q_ref, k_ref, v_ref, z_ref, ba_ref, cwq_ref, cwk_ref, cwv_ref, gpar_ref, onw_ref, o_ref,
               xpad_s, gb_s, qn_s, kn_s, vv_s, dlt_s, lev_s, u_s, wq_s, atk_s, gl_s, os_s, st_s):
    c = DN_C
    nc = DN_NC
    wp = DN_HP * DN_D
    ba = ba_ref[...]
    lane = lax.broadcasted_iota(jnp.int32, (1, 128), 1)
    xg = ba + gpar_ref[0, 0, 1:2, :]
    sp = jnp.maximum(xg, 0.0) + jnp.log1p(jnp.exp(-jnp.abs(xg)))
    gb_s[...] = jnp.where(lane < 4, _sigmoid(ba), -jnp.exp(gpar_ref[0, 0, 0:1, :]) * sp)

    ri = lax.broadcasted_iota(jnp.int32, (c, c), 0)
    ci = lax.broadcasted_iota(jnp.int32, (c, c), 1)
    dlt_s[...] = ri - ci
    xr = jnp.bitwise_xor(ri, ci)
    lev = jnp.zeros((c, c), jnp.int32)
    bs = 1
    while bs < c:
        lev = lev + (xr >= bs).astype(jnp.int32)
        bs *= 2
    lev_s[...] = lev
    n_lev = int(math.log2(c))

    zpad = jnp.zeros((8, wp), F32)
    xpad_s[0:8, :] = zpad
    xpad_s[S + 8:S + 16, :] = zpad

    def conv_silu(x_ref, cw_ref):
        xpad_s[8:S + 8, :] = x_ref[...].astype(F32)
        acc = None
        for t in range(DN_CONV):
            o0 = 8 + t - DN_CONV // 2
            term = xpad_s[o0:o0 + S, :] * cw_ref[0, t:t + 1, :]
            acc = term if acc is None else acc + term
        return _silu(acc)

    qa = conv_silu(q_ref, cwq_ref)
    for hh in range(DN_HP):
        l = slice(hh * DN_D, (hh + 1) * DN_D)
        qh = qa[:, l]
        qn_s[:, l] = qh * lax.rsqrt(jnp.sum(qh * qh, axis=-1, keepdims=True) + 1e-6) * (DN_D ** -0.5)
    ka = conv_silu(k_ref, cwk_ref)
    for hh in range(DN_HP):
        l = slice(hh * DN_D, (hh + 1) * DN_D)
        kh = ka[:, l]
        kn_s[:, l] = kh * lax.rsqrt(jnp.sum(kh * kh, axis=-1, keepdims=True) + 1e-6)
    vv_s[...] = conv_silu(v_ref, cwv_ref)

    def bdiag(a, b):
        za = jnp.zeros_like(a)
        return jnp.concatenate([jnp.concatenate([a, za], axis=1), jnp.concatenate([za, b], axis=1)], axis=0)

    def prep(i, carry):
        lv = lev_s[...]
        dlt = dlt_s[...]
        lv2 = jnp.concatenate([lv, lv], axis=1)
        incl = (dlt >= 0, dlt <= 0)
        strict = (dlt > 0, dlt < 0)
        tri = tuple(jnp.where(m, 1.0, 0.0).astype(BF) for m in incl)
        eye = jnp.where(lv == 0, 1.0, 0.0).astype(BF)

        cells = []
        for j in range(DN_UNROLL):
            n = i * DN_UNROLL + j
            r0 = pl.multiple_of(n * c, c)
            for hh in range(DN_HP):
                l = slice(hh * DN_D, (hh + 1) * DN_D)
                qc = qn_s[pl.ds(r0, c), l]
                kc = kn_s[pl.ds(r0, c), l]
                vc = vv_s[pl.ds(r0, c), l]
                gbc = gb_s[pl.ds(r0, c), :]
                beta = [jnp.broadcast_to(gbc[:, 2 * d + hh:2 * d + hh + 1], (c, c)) for d in range(2)]
                g = [jnp.broadcast_to(gbc[:, 4 + 2 * d + hh:5 + 2 * d + hh], (c, c)) for d in range(2)]
                kb = [kc * beta[d] for d in range(2)]
                lhs = jnp.concatenate([qc.astype(BF), kb[0].astype(BF), kb[1].astype(BF)], axis=0)
                prod = _dot_nt(lhs, kc.astype(BF))
                cells.append(dict(n=n, r0=r0, qc=qc, kc=kc, vc=vc, beta=beta, g=g, kb=kb, prod=prod))

        cum = []
        for d in range(2):
            rg = jnp.concatenate(
                [jnp.concatenate([ce["g"][d], jnp.where(strict[d], ce["g"][d], 0.0)], axis=1) for ce in cells],
                axis=1)
            cum.append(sum(_dot(tri[d], part) for part in _split3(rg)))

        items = []
        for j in range(DN_UNROLL):
            for d in range(2):
                low, rhs, at, qdec, kdec, gl = [], [], [], [], [], []
                for hh in range(DN_HP):
                    ce = cells[j * DN_HP + hh]
                    cd = cum[d][:, (j * DN_HP + hh) * 2 * c:(j * DN_HP + hh + 1) * 2 * c]
                    gc_b = cd[:, :c]
                    decay = jnp.where(incl[d], jnp.exp(jnp.where(incl[d], cd[:, c:], 0.0)), 0.0)
                    glast = gc_b[c - 1:c, :] if d == 0 else gc_b[0:1, :]
                    e_gc = jnp.exp(gc_b)
                    low.append(jnp.where(strict[d], ce["prod"][(1 + d) * c:(2 + d) * c] * decay, 0.0))
                    at.append(jnp.where(incl[d], ce["prod"][0:c] * decay, 0.0).astype(BF))
                    qdec.append((ce["qc"] * e_gc).astype(BF))
                    kdec.append((ce["kc"] * jnp.exp(glast - gc_b)).astype(BF))
                    rhs.append(jnp.concatenate([ce["vc"] * ce["beta"][d], ce["kb"][d] * e_gc], axis=1).astype(BF))
                    gl.append(jnp.exp(glast))
                lowp = jnp.concatenate(low, axis=1)
                n = cells[j * DN_HP]["n"]
                r0 = cells[j * DN_HP]["r0"]
                wq_s[d, n, c:2 * c, :] = jnp.concatenate(qdec, axis=1)
                atk_s[d, n, 0:c, :] = jnp.concatenate(at, axis=1)
                atk_s[d, n, c:2 * c, :] = _dot_nt(eye, jnp.concatenate(kdec, axis=0)).astype(BF)
                gl_s[d, n] = jnp.broadcast_to(jnp.concatenate(gl, axis=1), (8, wp))
                x = jnp.where(lv2 == 0, 1.0, 0.0) - jnp.where(lv2 == 1, lowp, 0.0)
                items.append(dict(d=d, n=n, r0=r0, lowp=lowp, rhs=rhs, x=x))

        for k in range(2, n_lev + 1):
            xb = [it["x"].astype(BF) for it in items]
            t1 = []
            for it, b in zip(items, xb):
                e = jnp.where(lv2 == k, it["lowp"], 0.0).astype(BF)
                t1.append(_dot(b, bdiag(e[:, :c], e[:, c:])).astype(BF))
            t2 = [_dot(t, bdiag(b[:, :c], b[:, c:])) for t, b in zip(t1, xb)]
            for it, t in zip(items, t2):
                it["x"] = it["x"] - t

        sols = [_dot(it["x"].astype(BF), bdiag(it["rhs"][0], it["rhs"][1])) for it in items]
        for it, sol in zip(items, sols):
            d, n, r0 = it["d"], it["n"], it["r0"]
            u_s[d, pl.ds(r0, c), :] = jnp.concatenate([sol[:, 0:DN_D], sol[:, 2 * DN_D:3 * DN_D]], axis=1)
            wq_s[d, n, 0:c, :] = jnp.concatenate([sol[:, DN_D:2 * DN_D], sol[:, 3 * DN_D:]], axis=1).astype(BF)
        return carry

    lax.fori_loop(0, nc // DN_UNROLL, prep, 0)

    st_s[...] = jnp.zeros((2, DN_D, wp), F32)

    def scan(i, carry):
        ns = (i, nc - 1 - i)
        sts = [st_s[d] for d in range(2)]
        prs = []
        for d in range(2):
            sb = sts[d].astype(BF)
            prs.append(_dot(wq_s[d, ns[d]], bdiag(sb[:, :DN_D], sb[:, DN_D:])))
        r2s = []
        for d in range(2):
            r0 = pl.multiple_of(ns[d] * c, c)
            vn = (u_s[d, pl.ds(r0, c), :] - prs[d][:c]).astype(BF)
            r2s.append(_dot(atk_s[d, ns[d]], bdiag(vn[:, :DN_D], vn[:, DN_D:])))
        for d in range(2):
            r0 = pl.multiple_of(ns[d] * c, c)
            os_s[d, pl.ds(r0, c), :] = prs[d][c:] + r2s[d][:c]
            st_s[d] = sts[d] * gl_s[d, ns[d]][0:1, :] + r2s[d][c:]
        return carry

    lax.fori_loop(0, nc, scan, 0)

    o = os_s[0] + os_s[1]
    sil = _silu(z_ref[...].astype(F32))
    for hh in range(DN_HP):
        l = slice(hh * DN_D, (hh + 1) * DN_D)
        o_ref[:, l] = (_rms(o[:, l], onw_ref[0]) * sil[:, l]).astype(BF)


def _deltanet(l, qkv, z, ba, conv_w, gpar, onw):
    c, nc, wp = DN_C, DN_NC, DN_HP * DN_D
    npair = DN_H // DN_HP
    col = lambda off: pl.BlockSpec((S, wp), lambda b, p: (b, off + p))
    cwcol = lambda off: pl.BlockSpec((1, DN_CONV, wp), lambda b, p: (l, 0, off + p))
    return pl.pallas_call(
        _dn_kernel,
        grid=(B, npair),
        in_specs=[col(0), col(npair), col(2 * npair), col(0),
                  pl.BlockSpec((S, 128), lambda b, p: (b, p)),
                  cwcol(0), cwcol(npair), cwcol(2 * npair),
                  pl.BlockSpec((1, 1, 2, 128), lambda b, p: (l, p, 0, 0)), _layer_spec(l, (1, DN_D))],
        out_specs=col(0),
        out_shape=jax.ShapeDtypeStruct((T, DN_W), BF),
        scratch_shapes=[pltpu.VMEM((S + 16, wp), F32), pltpu.VMEM((S, 128), F32),
                        pltpu.VMEM((S, wp), F32), pltpu.VMEM((S, wp), F32), pltpu.VMEM((S, wp), F32),
                        pltpu.VMEM((c, c), jnp.int32), pltpu.VMEM((c, c), jnp.int32),
                        pltpu.VMEM((2, S, wp), F32),
                        pltpu.VMEM((2, nc, 2 * c, wp), BF), pltpu.VMEM((2, nc, 2 * c, wp), BF),
                        pltpu.VMEM((2, nc, 8, wp), F32),
                        pltpu.VMEM((2, S, wp), F32), pltpu.VMEM((2, DN_D, wp), F32)],
        compiler_params=_cparams(("parallel", "parallel")),
        name="deltanet",
    )(qkv, qkv, qkv, z, ba, conv_w, conv_w, conv_w, gpar, onw)


TM_MG = 512


def _merge_kernel(oa_ref, ob_ref, oc_ref, gate_ref, bg_ref, wbr_ref, wo_ref, x_ref, nw1_ref, gt1_ref,
                  nw2_ref, sc2_ref, sh2_ref, x1_ref, h2_ref):
    acc = None
    for n, o_ref in enumerate((oa_ref, ob_ref, oc_ref)):
        up = _dot(o_ref[...], wbr_ref[0, n])
        g = _sigmoid(gate_ref[:, n * D:(n + 1) * D].astype(F32) + bg_ref[0, n:n + 1, :])
        acc = g * up if acc is None else acc + g * up
    y = _dot(acc.astype(BF), wo_ref[0])
    x1 = x_ref[...] + gt1_ref[0] * _rms(y, nw1_ref[0])
    x1_ref[...] = x1
    h2_ref[...] = (_rms(x1, nw2_ref[0]) * (1.0 + sc2_ref[0]) + sh2_ref[0]).astype(BF)


def _merge(l, oa, ob, oc, gate, bg, wbr, wo, x, mod3, nw):
    tm = TM_MG
    per_b = S // tm
    row = lambda n: pl.BlockSpec((tm, n), lambda i: (i, 0))
    modspec = lambda k: pl.BlockSpec((1, 1, D), lambda i: (l * B * 6 + (i // per_b) * 6 + k, 0, 0))
    return pl.pallas_call(
        _merge_kernel,
        grid=(T // tm,),
        in_specs=[row(512), row(512), row(512), row(3 * D), _layer_spec(l, (3, D)),
                  _layer_spec(l, (3, 512, D)), _layer_spec(l, (D, D)), row(D), _norm_spec(l, 1),
                  modspec(2), _norm_spec(l, 2), modspec(4), modspec(3)],
        out_specs=[row(D), row(D)],
        out_shape=[jax.ShapeDtypeStruct((T, D), F32), jax.ShapeDtypeStruct((T, D), BF)],
        compiler_params=_cparams(("parallel",)),
        name="merge",
    )(oa, ob, oc, gate, bg, wbr, wo, x, nw, mod3, nw, mod3, mod3)


TM_FF = 512
TF = 256
FF_HALO = 16


def _ffn_kernel(h_ref, x1_ref, wup_ref, cw_ref, cb_ref, wdn_ref, nw_ref, gt_ref, o_ref, act_s):
    i = pl.program_id(1)
    nblk = S // TM_FF
    r0 = pl.multiple_of(i * TM_FF, TM_FF)
    top0 = pl.multiple_of(jnp.maximum(r0 - FF_HALO, 0), FF_HALO)
    bot0 = pl.multiple_of(jnp.minimum(r0 + TM_FF, S - FF_HALO), FF_HALO)
    zero = jnp.zeros((FF_HALO, D), BF)
    top = jnp.where(i > 0, h_ref[pl.ds(top0, FF_HALO), :], zero)
    bot = jnp.where(i < nblk - 1, h_ref[pl.ds(bot0, FF_HALO), :], zero)
    slab = jnp.concatenate([top, h_ref[pl.ds(r0, TM_FF), :], bot], axis=0)

    def up_conv(col):
        p = _dot(slab, wup_ref[0, :, col:col + TF])
        cw = cw_ref[0, :, col:col + TF]
        h0 = FF_HALO
        return (p[h0 - 1:h0 - 1 + TM_FF] * cw[0:1] + p[h0:h0 + TM_FF] * cw[1:2]
                + p[h0 + 1:h0 + 1 + TM_FF] * cw[2:3] + cb_ref[0, :, col:col + TF])

    for j in range(D_FF // TF):
        a = up_conv(j * TF)
        b = up_conv(D_FF + j * TF)
        act_s[:, j * TF:(j + 1) * TF] = (_silu(a) * b).astype(BF)
    y = _dot(act_s[...], wdn_ref[0])
    o_ref[...] = x1_ref[...] + gt_ref[0] * _rms(y, nw_ref[0])


def _ffn(l, h2, x1, w_up, conv_w, conv_b, w_down, mod3, nw):
    nblk = S // TM_FF
    row = pl.BlockSpec((TM_FF, D), lambda b, i: (b * nblk + i, 0))
    return pl.pallas_call(
        _ffn_kernel,
        grid=(B, nblk),
        in_specs=[pl.BlockSpec((S, D), lambda b, i: (b, 0)), row,
                  _layer_spec(l, (D, 2 * D_FF)), _layer_spec(l, (3, 2 * D_FF)), _layer_spec(l, (1, 2 * D_FF)),
                  _layer_spec(l, (D_FF, D)), _norm_spec(l, 3),
                  pl.BlockSpec((1, 1, D), lambda b, i: (l * B * 6 + b * 6 + 5, 0, 0))],
        out_specs=row,
        out_shape=jax.ShapeDtypeStruct((T, D), F32),
        scratch_shapes=[pltpu.VMEM((TM_FF, D_FF), BF)],
        compiler_params=_cparams(("parallel", "arbitrary")),
        name="conv_ffn",
    )(h2, x1, w_up, conv_w, conv_b, w_down, nw, mod3)


def _pair_lanes(t):
    t = t.reshape(t.shape[:-1] + (2, DN_H // DN_HP, DN_HP))
    t = jnp.moveaxis(t, -2, -3)
    return t.reshape(t.shape[:-2] + (2 * DN_HP,))


def _prep_w_in(w_in):
    a_uv, qkv, z, beta, alpha, cq, ckv, kr, gate = jnp.split(
        w_in, [1024, 2560, 3072, 3080, 3088, 3472, 3728, 3792], axis=-1)
    kr_sw = jnp.concatenate([kr[..., 32:], kr[..., :32]], axis=-1)
    bp = _pair_lanes(beta)
    ap = _pair_lanes(alpha)
    pad = jnp.zeros(w_in.shape[:-1] + (DN_H // DN_HP, 120), w_in.dtype)
    ba = jnp.concatenate([bp, ap, pad], axis=-1).reshape(w_in.shape[:-1] + (256,))
    return jnp.concatenate([a_uv, qkv, z, cq, ckv, kr, kr_sw, ba, gate], axis=-1).astype(BF)


def _prep_gpar(a_log, dt_bias):
    def lanes(t):
        tp = _pair_lanes(t.reshape(DEPTH, 2 * DN_H))
        z4 = jnp.zeros((DEPTH, DN_H // DN_HP, 4), F32)
        z120 = jnp.zeros((DEPTH, DN_H // DN_HP, 120), F32)
        return jnp.concatenate([z4, tp, z120], axis=-1)
    return jnp.stack([lanes(a_log), lanes(dt_bias)], axis=2)


def _prep_w_uq(w_uq):
    wq = w_uq.reshape(DEPTH, ML_QR, ML_H, ML_DQ)
    nope = wq[..., :ML_NOPE].reshape(DEPTH, ML_QR, ML_H * ML_NOPE)
    pe = wq[..., ML_NOPE:]
    pe_sw = jnp.concatenate([pe[..., 32:], pe[..., :32]], axis=-1)
    return jnp.concatenate([nope, pe.reshape(DEPTH, ML_QR, 256), pe_sw.reshape(DEPTH, ML_QR, 256)],
                           axis=-1).astype(BF)


def _prep_w_ukv(w_ukv):
    wk = w_ukv.reshape(DEPTH, ML_KVR, ML_H, ML_NOPE + ML_V)
    return jnp.concatenate([wk[..., :ML_NOPE].reshape(DEPTH, ML_KVR, 512),
                            wk[..., ML_NOPE:].reshape(DEPTH, ML_KVR, 512)], axis=-1).astype(BF)


@jax.jit
def _forward(x, c, positions, w_ada, b_ada, norm_w, w_in, b_gate, a_ln_w, a_ln_b, a_w_sp, a_b_sp, dn_conv_w,
             dn_a_log, dn_dt_bias, dn_o_norm_w, mla_q_norm_w, mla_kv_norm_w, mla_w_uq, mla_w_ukv, w_branch,
             w_o, ffn_w_up, ffn_conv_w, ffn_conv_b, ffn_w_down):
    mod = _modulation(c, w_ada, b_ada)
    cc, ss = _rope_tables(positions)
    w_in_r = _prep_w_in(w_in)
    wuq_r = _prep_w_uq(mla_w_uq)
    wukv_r = _prep_w_ukv(mla_w_ukv)
    wsp = a_w_sp.astype(BF)
    bsp = jnp.broadcast_to(a_b_sp[..., None], (DEPTH, 4, 128, 128))
    wbr = w_branch.astype(BF)
    wo = w_o.astype(BF)
    wup = ffn_w_up.astype(BF)
    wdn = ffn_w_down.astype(BF)
    gpar = _prep_gpar(dn_a_log, dn_dt_bias)
    mod3 = mod.reshape(DEPTH * B * 6, 1, D)
    lnw = a_ln_w.reshape(DEPTH, 1, A_W)
    lnb = a_ln_b.reshape(DEPTH, 1, A_W)
    qnw = mla_q_norm_w.reshape(DEPTH, 1, ML_QR)
    kvnw = mla_kv_norm_w.reshape(DEPTH, 1, ML_KVR)
    onw = dn_o_norm_w.reshape(DEPTH, 1, DN_D)
    cb = ffn_conv_b.reshape(DEPTH, 1, 2 * D_FF)
    nw = norm_w.reshape(DEPTH * 4, 1, D)

    xf = x.reshape(T, D)
    for l in range(DEPTH):
        oa, qkv, z, ba, gate, q, k, v = _in_proj(l, xf, mod3, nw, w_in_r, lnw, lnb, wsp, bsp, qnw, kvnw,
                                                 wuq_r, wukv_r, cc, ss)
        ob = _deltanet(l, qkv, z, ba, dn_conv_w, gpar, onw)
        oc = _attention(q, k, v)
        x1, h2 = _merge(l, oa, ob, oc, gate, b_gate, wbr, wo, xf, mod3, nw)
        xf = _ffn(l, h2, x1, wup, ffn_conv_w, cb, wdn, mod3, nw)
    return xf.reshape(B, S, D)


def kernel(x, c, positions, w_ada, b_ada, norm_w, w_in, b_gate, a_ln_w, a_ln_b, a_w_sp, a_b_sp, dn_conv_w,
           dn_a_log, dn_dt_bias, dn_o_norm_w, mla_q_norm_w, mla_kv_norm_w, mla_w_uq, mla_w_ukv, w_branch, w_o,
           ffn_w_up, ffn_conv_w, ffn_conv_b, ffn_w_down):
    return _forward(x, c, positions, w_ada, b_ada, norm_w, w_in, b_gate, a_ln_w, a_ln_b, a_w_sp, a_b_sp,
                    dn_conv_w, dn_a_log, dn_dt_bias, dn_o_norm_w, mla_q_norm_w, mla_kv_norm_w, mla_w_uq,
                    mla_w_ukv, w_branch, w_o, ffn_w_up, ffn_conv_w, ffn_conv_b, ffn_w_down)
```

```python
import math

import jax
import jax.numpy as jnp
from jax import lax
from jax.experimental import pallas as pl
from jax.experimental.pallas import tpu as pltpu

F32 = jnp.float32
BF = jnp.bfloat16

D = 1024
B = 8
S = 2048
DEPTH = 4
T = B * S

A_W = 512
A_CHUNK = 128
DN_H = 4
DN_D = 128
DN_W = 512
DN_CONV = 5
DN_C = 128
DN_NC = S // DN_C
DN_HP = 2
DN_UNROLL = 4
ML_H = 4
ML_QR = 384
ML_KVR = 256
ML_NOPE = 128
ML_ROPE = 64
ML_V = 128
ML_DQ = ML_NOPE + ML_ROPE
ROPE_THETA = 10000.0
D_FF = 2816
RMS_EPS = 1e-6
LN_EPS = 1e-5

C_A = 0
C_QKV = 1024
C_Z = 2560
C_CQ = 3072
C_CKV = 3456
C_KR = 3712
C_BA = 3840
C_GATE = 4096
NP_IN = 7168

VMEM_LIMIT = 56 * 1024 * 1024


def _cparams(sem):
    return pltpu.CompilerParams(dimension_semantics=sem, vmem_limit_bytes=VMEM_LIMIT)


def _dot(a, b):
    return jnp.dot(a, b, preferred_element_type=F32)


def _dot_nt(a, b):
    return lax.dot_general(a, b, (((1,), (1,)), ((), ())), preferred_element_type=F32)


def _rms(x, w, eps=RMS_EPS):
    return x * lax.rsqrt(jnp.mean(x * x, axis=-1, keepdims=True) + eps) * w


def _sigmoid(x):
    return 1.0 / (1.0 + jnp.exp(-x))


def _silu(x):
    return x * _sigmoid(x)


def _gelu_tanh(x):
    c = math.sqrt(2.0 / math.pi)
    return 0.5 * x * (1.0 + jnp.tanh(c * (x + 0.044715 * (x * x * x))))


def _layer_spec(l, tail, idx=None):
    idx = (0,) * len(tail) if idx is None else idx
    return pl.BlockSpec((1,) + tuple(tail), lambda *_: (l,) + tuple(idx), pipeline_mode=pl.Buffered(1))


def _norm_spec(l, k):
    return pl.BlockSpec((1, 1, D), lambda *_: (l * 4 + k, 0, 0), pipeline_mode=pl.Buffered(1))


def _mod_kernel(c_ref, w_ref, b_ref, o_ref):
    c = c_ref[...]
    ca = _silu(c).astype(BF)
    o_ref[0] = _dot(ca, w_ref[0].astype(BF)) + b_ref[0]


def _modulation(c, w_ada, b_ada):
    tn = 1536
    return pl.pallas_call(
        _mod_kernel,
        grid=(DEPTH, 6 * D // tn),
        in_specs=[pl.BlockSpec((B, D), lambda l, j: (0, 0)),
                  pl.BlockSpec((1, D, tn), lambda l, j: (l, 0, j)),
                  pl.BlockSpec((1, 1, tn), lambda l, j: (l, 0, j))],
        out_specs=pl.BlockSpec((1, B, tn), lambda l, j: (l, 0, j)),
        out_shape=jax.ShapeDtypeStruct((DEPTH, B, 6 * D), F32),
        compiler_params=_cparams(("parallel", "parallel")),
        name="adaln_mod",
    )(c, w_ada, b_ada.reshape(DEPTH, 1, 6 * D))


def _rope_kernel(pos_ref, invf_ref, sign_ref, cc_ref, ss_ref):
    ang = pos_ref[...].astype(F32) * invf_ref[...]
    c = jnp.cos(ang)
    s = jnp.sin(ang) * sign_ref[...]
    cc_ref[...] = jnp.concatenate([c, c], axis=-1)
    ss_ref[...] = jnp.concatenate([s, s], axis=-1)


def _rope_tables(positions):
    half = ML_ROPE // 2
    inv_freq = ROPE_THETA ** (-jnp.arange(0, ML_ROPE, 2, dtype=F32) / ML_ROPE)
    invf = jnp.tile(inv_freq, 4).reshape(1, 128)
    sign = jnp.tile(jnp.concatenate([-jnp.ones((half,), F32), jnp.ones((half,), F32)]), 2).reshape(1, 128)
    return pl.pallas_call(
        _rope_kernel,
        grid=(B,),
        in_specs=[pl.BlockSpec((S, 1), lambda b: (b, 0)),
                  pl.BlockSpec((1, 128), lambda b: (0, 0)),
                  pl.BlockSpec((1, 128), lambda b: (0, 0))],
        out_specs=[pl.BlockSpec((S, 256), lambda b: (b, 0)),
                   pl.BlockSpec((S, 256), lambda b: (b, 0))],
        out_shape=[jax.ShapeDtypeStruct((T, 256), F32), jax.ShapeDtypeStruct((T, 256), F32)],
        compiler_params=_cparams(("parallel",)),
        name="rope_tables",
    )(positions.reshape(T, 1), invf, sign)


TM_IN = 512
IN_HALO = 16


def _in_kernel(x_ref, xt_ref, xb_ref, nw_ref, sc_ref, sh_ref, w_ref, lnw_ref, lnb_ref, wsp_ref, bsp_ref,
               qnw_ref, kvnw_ref, wuq_ref, wukv_ref, cc_ref, ss_ref, dcw_ref,
               oa_ref, qkv_ref, z_ref, ba_ref, gate_ref, q_ref, k_ref, v_ref):
    i = pl.program_id(0)
    per_b = S // TM_IN

    def modulate(xx):
        return (_rms(xx, nw_ref[0]) * (1.0 + sc_ref[0]) + sh_ref[0]).astype(BF)

    hb = modulate(x_ref[...])

    def seg(c0, n):
        return _dot(hb, w_ref[0, :, c0:c0 + n])

    zero = jnp.zeros((IN_HALO, D), BF)
    top = jnp.where(i % per_b > 0, modulate(xt_ref[...]), zero)
    bot = jnp.where(i % per_b < per_b - 1, modulate(xb_ref[...]), zero)
    slab = jnp.concatenate([top, hb, bot], axis=0)
    for part in range(3):
        c0 = part * DN_W
        p = _dot(slab, w_ref[0, :, C_QKV + c0:C_QKV + c0 + DN_W])
        acc = None
        for t in range(DN_CONV):
            o0 = IN_HALO + t - DN_CONV // 2
            term = p[o0:o0 + TM_IN] * dcw_ref[0, t:t + 1, c0:c0 + DN_W]
            acc = term if acc is None else acc + term
        y = _silu(acc)
        for hh in range(DN_H):
            l = slice(hh * DN_D, (hh + 1) * DN_D)
            yh = y[:, l]
            if part == 0:
                yh = yh * (lax.rsqrt(jnp.sum(yh * yh, axis=-1, keepdims=True) + 1e-6) * (DN_D ** -0.5))
            elif part == 1:
                yh = yh * lax.rsqrt(jnp.sum(yh * yh, axis=-1, keepdims=True) + 1e-6)
            qkv_ref[:, c0 + hh * DN_D:c0 + (hh + 1) * DN_D] = yh.astype(BF)

    a = _gelu_tanh(seg(C_A, 2 * A_W))
    u = a[:, :A_W]
    v = a[:, A_W:]
    mu = jnp.mean(v, axis=-1, keepdims=True)
    vc = v - mu
    var = jnp.mean(vc * vc, axis=-1, keepdims=True)
    vn = (vc * lax.rsqrt(var + LN_EPS) * lnw_ref[0] + lnb_ref[0]).astype(BF)
    for c in range(TM_IN // A_CHUNK):
        r = slice(c * A_CHUNK, (c + 1) * A_CHUNK)
        for g in range(4):
            l = slice(g * 128, (g + 1) * 128)
            mixed = _dot(wsp_ref[0, g], vn[r, l]) + bsp_ref[0, g]
            oa_ref[r, l] = (u[r, l] * mixed).astype(BF)

    z_ref[...] = seg(C_Z, DN_W).astype(BF)
    ba_ref[...] = seg(C_BA, 256)
    gate_ref[...] = seg(C_GATE, 3 * D).astype(BF)

    cc = cc_ref[...]
    ss = ss_ref[...]
    cqn = _rms(seg(C_CQ, ML_QR), qnw_ref[0]).astype(BF)
    qall = _dot(cqn, wuq_ref[0]) * (ML_DQ ** -0.5)
    q_rot = qall[:, 512:768] * cc + qall[:, 768:1024] * ss
    ckvn = _rms(seg(C_CKV, ML_KVR), kvnw_ref[0]).astype(BF)
    kvall = _dot(ckvn, wukv_ref[0])
    krr = seg(C_KR, 128)
    k_pe = (krr[:, :64] * cc[:, :64] + krr[:, 64:] * ss[:, :64]).astype(BF)
    for hh in range(ML_H):
        l = slice(hh * 128, (hh + 1) * 128)
        q_ref[0, hh, :, 0:ML_NOPE] = qall[:, l].astype(BF)
        q_ref[0, hh, :, ML_NOPE:ML_DQ] = q_rot[:, hh * 64:(hh + 1) * 64].astype(BF)
        k_ref[0, hh, :, 0:ML_NOPE] = kvall[:, l].astype(BF)
        k_ref[0, hh, :, ML_NOPE:ML_DQ] = k_pe
        v_ref[0, hh] = kvall[:, 512 + hh * 128:512 + (hh + 1) * 128].astype(BF)


def _in_proj(l, x, mod3, nw, w_in, lnw, lnb, wsp, bsp, qnw, kvnw, wuq, wukv, cc, ss, dcw):
    tm = TM_IN
    per_b = S // tm
    nh = tm // IN_HALO
    row = lambda n: pl.BlockSpec((tm, n), lambda i: (i, 0))
    modspec = lambda k: pl.BlockSpec((1, 1, D), lambda i: (l * B * 6 + (i // per_b) * 6 + k, 0, 0))
    hspec = lambda n: pl.BlockSpec((1, ML_H, tm, n), lambda i: (i // per_b, 0, i % per_b, 0))
    return pl.pallas_call(
        _in_kernel,
        grid=(T // tm,),
        in_specs=[row(D),
                  pl.BlockSpec((IN_HALO, D), lambda i: (jnp.maximum(i * nh - 1, 0), 0)),
                  pl.BlockSpec((IN_HALO, D), lambda i: (jnp.minimum((i + 1) * nh, T // IN_HALO - 1), 0)),
                  _norm_spec(l, 0), modspec(1), modspec(0), _layer_spec(l, (D, NP_IN)),
                  _layer_spec(l, (1, A_W)), _layer_spec(l, (1, A_W)), _layer_spec(l, (4, 128, 128)),
                  _layer_spec(l, (4, 128, 128)), _layer_spec(l, (1, ML_QR)), _layer_spec(l, (1, ML_KVR)),
                  _layer_spec(l, (ML_QR, 1024)), _layer_spec(l, (ML_KVR, 1024)), row(256), row(256),
                  _layer_spec(l, (DN_CONV, 3 * DN_W))],
        out_specs=[row(A_W), row(3 * DN_W), row(DN_W), row(256), row(3 * D),
                   hspec(ML_DQ), hspec(ML_DQ), hspec(ML_V)],
        out_shape=[jax.ShapeDtypeStruct((T, A_W), BF), jax.ShapeDtypeStruct((T, 3 * DN_W), BF),
                   jax.ShapeDtypeStruct((T, DN_W), BF), jax.ShapeDtypeStruct((T, 256), F32),
                   jax.ShapeDtypeStruct((T, 3 * D), BF),
                   jax.ShapeDtypeStruct((B, ML_H, S, ML_DQ), BF),
                   jax.ShapeDtypeStruct((B, ML_H, S, ML_DQ), BF),
                   jax.ShapeDtypeStruct((B, ML_H, S, ML_V), BF)],
        compiler_params=_cparams(("parallel",)),
        name="in_proj",
    )(x, x, x, nw, mod3, mod3, w_in, lnw, lnb, wsp, bsp, qnw, kvnw, wuq, wukv, cc, ss, dcw)


TQ = 2048
TQ_SUB = 256


def _attn_kernel(q_ref, k_ref, v_ref, o_ref):
    k = k_ref[0, 0]
    v = v_ref[0, 0]
    nsub = TQ // TQ_SUB

    def scores(i):
        return _dot_nt(q_ref[0, 0, i * TQ_SUB:(i + 1) * TQ_SUB, :], k)

    s_next = scores(0)
    for i in range(nsub):
        s = s_next
        if i + 1 < nsub:
            s_next = scores(i + 1)
        m = jnp.max(s, axis=-1, keepdims=True)
        p = jnp.exp(s - m)
        l = jnp.sum(p, axis=-1, keepdims=True)
        o = _dot(p.astype(BF), v)
        o_ref[i * TQ_SUB:(i + 1) * TQ_SUB, :] = (o / l).astype(BF)


def _attention(q, k, v):
    nq = S // TQ
    return pl.pallas_call(
        _attn_kernel,
        grid=(B, ML_H, nq),
        in_specs=[pl.BlockSpec((1, 1, TQ, ML_DQ), lambda b, h, i: (b, h, i, 0)),
                  pl.BlockSpec((1, 1, S, ML_DQ), lambda b, h, i: (b, h, 0, 0)),
                  pl.BlockSpec((1, 1, S, ML_V), lambda b, h, i: (b, h, 0, 0))],
        out_specs=pl.BlockSpec((TQ, ML_V), lambda b, h, i: (b * nq + i, h)),
        out_shape=jax.ShapeDtypeStruct((T, ML_H * ML_V), BF),
        compiler_params=_cparams(("parallel", "parallel", "parallel")),
        name="mla_attn",
    )(q, k, v)


def _split3(x):
    hi = x.astype(BF)
    r = x - hi.astype(F32)
    mid = r.astype(BF)
    lo = (r - mid.astype(F32)).astype(BF)
    return hi, mid, lo


def _dn_kernel(q_ref, k_ref, v_ref, z_ref, ba_ref, gpar_ref, onw_ref, o_ref,
               gb_s, dlt_s, lev_s, u_s, wq_s, atk_s, gl_s, os_s, st_s):
    c = DN_C
    nc = DN_NC
    wp = DN_HP * DN_D
    ba = ba_ref[...]
    lane = lax.broadcasted_iota(jnp.int32, (1, 128), 1)
    xg = ba + gpar_ref[0, 0, 1:2, :]
    sp = jnp.maximum(xg, 0.0) + jnp.log1p(jnp.exp(-jnp.abs(xg)))
    gb_s[...] = jnp.where(lane < 4, _sigmoid(ba), -jnp.exp(gpar_ref[0, 0, 0:1, :]) * sp)

    ri = lax.broadcasted_iota(jnp.int32, (c, c), 0)
    ci = lax.broadcasted_iota(jnp.int32, (c, c), 1)
    dlt_s[...] = ri - ci
    xr = jnp.bitwise_xor(ri, ci)
    lev = jnp.zeros((c, c), jnp.int32)
    bs = 1
    while bs < c:
        lev = lev + (xr >= bs).astype(jnp.int32)
        bs *= 2
    lev_s[...] = lev
    n_lev = int(math.log2(c))

    def bdiag(a, b):
        za = jnp.zeros_like(a)
        return jnp.concatenate([jnp.concatenate([a, za], axis=1), jnp.concatenate([za, b], axis=1)], axis=0)

    def prep(i, carry):
        lv = lev_s[...]
        dlt = dlt_s[...]
        lv2 = jnp.concatenate([lv, lv], axis=1)
        incl = (dlt >= 0, dlt <= 0)
        strict = (dlt > 0, dlt < 0)
        tri = tuple(jnp.where(m, 1.0, 0.0).astype(BF) for m in incl)
        eye = jnp.where(lv == 0, 1.0, 0.0).astype(BF)

        cells = []
        for j in range(DN_UNROLL):
            n = i * DN_UNROLL + j
            r0 = pl.multiple_of(n * c, c)
            for hh in range(DN_HP):
                l = slice(hh * DN_D, (hh + 1) * DN_D)
                qcb = q_ref[pl.ds(r0, c), l]
                kcb = k_ref[pl.ds(r0, c), l]
                qc = qcb.astype(F32)
                kc = kcb.astype(F32)
                vc = v_ref[pl.ds(r0, c), l].astype(F32)
                gbc = gb_s[pl.ds(r0, c), :]
                beta = [jnp.broadcast_to(gbc[:, 2 * d + hh:2 * d + hh + 1], (c, c)) for d in range(2)]
                g = [jnp.broadcast_to(gbc[:, 4 + 2 * d + hh:5 + 2 * d + hh], (c, c)) for d in range(2)]
                kb = [kc * beta[d] for d in range(2)]
                lhs = jnp.concatenate([qcb, kb[0].astype(BF), kb[1].astype(BF)], axis=0)
                prod = _dot_nt(lhs, kcb)
                cells.append(dict(n=n, r0=r0, qc=qc, kc=kc, vc=vc, beta=beta, g=g, kb=kb, prod=prod))

        cum = []
        for d in range(2):
            rg = jnp.concatenate(
                [jnp.concatenate([ce["g"][d], jnp.where(strict[d], ce["g"][d], 0.0)], axis=1) for ce in cells],
                axis=1)
            cum.append(sum(_dot(tri[d], part) for part in _split3(rg)))

        items = []
        for j in range(DN_UNROLL):
            for d in range(2):
                low, rhs, at, qdec, kdec, gl = [], [], [], [], [], []
                for hh in range(DN_HP):
                    ce = cells[j * DN_HP + hh]
                    cd = cum[d][:, (j * DN_HP + hh) * 2 * c:(j * DN_HP + hh + 1) * 2 * c]
                    gc_b = cd[:, :c]
                    decay = jnp.where(incl[d], jnp.exp(jnp.where(incl[d], cd[:, c:], 0.0)), 0.0)
                    glast = gc_b[c - 1:c, :] if d == 0 else gc_b[0:1, :]
                    e_gc = jnp.exp(gc_b)
                    low.append(jnp.where(strict[d], ce["prod"][(1 + d) * c:(2 + d) * c] * decay, 0.0))
                    at.append(jnp.where(incl[d], ce["prod"][0:c] * decay, 0.0).astype(BF))
                    qdec.append((ce["qc"] * e_gc).astype(BF))
                    kdec.append((ce["kc"] * jnp.exp(glast - gc_b)).astype(BF))
                    rhs.append(jnp.concatenate([ce["vc"] * ce["beta"][d], ce["kb"][d] * e_gc], axis=1).astype(BF))
                    gl.append(jnp.exp(glast))
                lowp = jnp.concatenate(low, axis=1)
                n = cells[j * DN_HP]["n"]
                r0 = cells[j * DN_HP]["r0"]
                wq_s[d, n, c:2 * c, :] = jnp.concatenate(qdec, axis=1)
                atk_s[d, n, 0:c, :] = jnp.concatenate(at, axis=1)
                atk_s[d, n, c:2 * c, :] = _dot_nt(eye, jnp.concatenate(kdec, axis=0)).astype(BF)
                gl_s[d, n] = jnp.broadcast_to(jnp.concatenate(gl, axis=1), (8, wp))
                x = jnp.where(lv2 == 0, 1.0, 0.0) - jnp.where(lv2 == 1, lowp, 0.0)
                items.append(dict(d=d, n=n, r0=r0, lowp=lowp, rhs=rhs, x=x))

        for k in range(2, n_lev + 1):
            xb = [it["x"].astype(BF) for it in items]
            t1 = []
            for it, b in zip(items, xb):
                e = jnp.where(lv2 == k, it["lowp"], 0.0).astype(BF)
                t1.append(_dot(b, bdiag(e[:, :c], e[:, c:])).astype(BF))
            t2 = [_dot(t, bdiag(b[:, :c], b[:, c:])) for t, b in zip(t1, xb)]
            for it, t in zip(items, t2):
                it["x"] = it["x"] - t

        sols = [_dot(it["x"].astype(BF), bdiag(it["rhs"][0], it["rhs"][1])) for it in items]
        for it, sol in zip(items, sols):
            d, n, r0 = it["d"], it["n"], it["r0"]
            u_s[d, pl.ds(r0, c), :] = jnp.concatenate([sol[:, 0:DN_D], sol[:, 2 * DN_D:3 * DN_D]], axis=1)
            wq_s[d, n, 0:c, :] = jnp.concatenate([sol[:, DN_D:2 * DN_D], sol[:, 3 * DN_D:]], axis=1).astype(BF)
        return carry

    lax.fori_loop(0, nc // DN_UNROLL, prep, 0)

    st_s[...] = jnp.zeros((2, DN_D, wp), F32)

    def scan(i, carry):
        ns = (i, nc - 1 - i)
        sts = [st_s[d] for d in range(2)]
        prs = []
        for d in range(2):
            sb = sts[d].astype(BF)
            prs.append(_dot(wq_s[d, ns[d]], bdiag(sb[:, :DN_D], sb[:, DN_D:])))
        r2s = []
        for d in range(2):
            r0 = pl.multiple_of(ns[d] * c, c)
            vn = (u_s[d, pl.ds(r0, c), :] - prs[d][:c]).astype(BF)
            r2s.append(_dot(atk_s[d, ns[d]], bdiag(vn[:, :DN_D], vn[:, DN_D:])))
        for d in range(2):
            r0 = pl.multiple_of(ns[d] * c, c)
            os_s[d, pl.ds(r0, c), :] = prs[d][c:] + r2s[d][:c]
            st_s[d] = sts[d] * gl_s[d, ns[d]][0:1, :] + r2s[d][c:]
        return carry

    lax.fori_loop(0, nc, scan, 0)

    o = os_s[0] + os_s[1]
    sil = _silu(z_ref[...].astype(F32))
    for hh in range(DN_HP):
        l = slice(hh * DN_D, (hh + 1) * DN_D)
        o_ref[:, l] = (_rms(o[:, l], onw_ref[0]) * sil[:, l]).astype(BF)


def _deltanet(l, qkv, z, ba, gpar, onw):
    c, nc, wp = DN_C, DN_NC, DN_HP * DN_D
    npair = DN_H // DN_HP
    col = lambda off: pl.BlockSpec((S, wp), lambda b, p: (b, off + p))
    return pl.pallas_call(
        _dn_kernel,
        grid=(B, npair),
        in_specs=[col(0), col(npair), col(2 * npair), col(0),
                  pl.BlockSpec((S, 128), lambda b, p: (b, p)),
                  pl.BlockSpec((1, 1, 2, 128), lambda b, p: (l, p, 0, 0)), _layer_spec(l, (1, DN_D))],
        out_specs=col(0),
        out_shape=jax.ShapeDtypeStruct((T, DN_W), BF),
        scratch_shapes=[pltpu.VMEM((S, 128), F32),
                        pltpu.VMEM((c, c), jnp.int32), pltpu.VMEM((c, c), jnp.int32),
                        pltpu.VMEM((2, S, wp), F32),
                        pltpu.VMEM((2, nc, 2 * c, wp), BF), pltpu.VMEM((2, nc, 2 * c, wp), BF),
                        pltpu.VMEM((2, nc, 8, wp), F32),
                        pltpu.VMEM((2, S, wp), F32), pltpu.VMEM((2, DN_D, wp), F32)],
        compiler_params=_cparams(("parallel", "parallel")),
        name="deltanet",
    )(qkv, qkv, qkv, z, ba, gpar, onw)


TM_MG = 512


def _merge_kernel(oa_ref, ob_ref, oc_ref, gate_ref, bg_ref, wbr_ref, wo_ref, x_ref, nw1_ref, gt1_ref,
                  nw2_ref, sc2_ref, sh2_ref, x1_ref, h2_ref):
    acc = None
    for n, o_ref in enumerate((oa_ref, ob_ref, oc_ref)):
        up = _dot(o_ref[...], wbr_ref[0, n])
        g = _sigmoid(gate_ref[:, n * D:(n + 1) * D].astype(F32) + bg_ref[0, n:n + 1, :])
        acc = g * up if acc is None else acc + g * up
    y = _dot(acc.astype(BF), wo_ref[0])
    x1 = x_ref[...] + gt1_ref[0] * _rms(y, nw1_ref[0])
    x1_ref[...] = x1
    h2_ref[...] = (_rms(x1, nw2_ref[0]) * (1.0 + sc2_ref[0]) + sh2_ref[0]).astype(BF)


def _merge(l, oa, ob, oc, gate, bg, wbr, wo, x, mod3, nw):
    tm = TM_MG
    per_b = S // tm
    row = lambda n: pl.BlockSpec((tm, n), lambda i: (i, 0))
    modspec = lambda k: pl.BlockSpec((1, 1, D), lambda i: (l * B * 6 + (i // per_b) * 6 + k, 0, 0))
    return pl.pallas_call(
        _merge_kernel,
        grid=(T // tm,),
        in_specs=[row(512), row(512), row(512), row(3 * D), _layer_spec(l, (3, D)),
                  _layer_spec(l, (3, 512, D)), _layer_spec(l, (D, D)), row(D), _norm_spec(l, 1),
                  modspec(2), _norm_spec(l, 2), modspec(4), modspec(3)],
        out_specs=[row(D), row(D)],
        out_shape=[jax.ShapeDtypeStruct((T, D), F32), jax.ShapeDtypeStruct((T, D), BF)],
        compiler_params=_cparams(("parallel",)),
        name="merge",
    )(oa, ob, oc, gate, bg, wbr, wo, x, nw, mod3, nw, mod3, mod3)


TM_FF = 512
TF = 256
FF_HALO = 16


def _ffn_kernel(h_ref, x1_ref, wup_ref, cw_ref, cb_ref, wdn_ref, nw_ref, gt_ref, o_ref, act_s):
    i = pl.program_id(1)
    nblk = S // TM_FF
    r0 = pl.multiple_of(i * TM_FF, TM_FF)
    top0 = pl.multiple_of(jnp.maximum(r0 - FF_HALO, 0), FF_HALO)
    bot0 = pl.multiple_of(jnp.minimum(r0 + TM_FF, S - FF_HALO), FF_HALO)
    zero = jnp.zeros((FF_HALO, D), BF)
    top = jnp.where(i > 0, h_ref[pl.ds(top0, FF_HALO), :], zero)
    bot = jnp.where(i < nblk - 1, h_ref[pl.ds(bot0, FF_HALO), :], zero)
    slab = jnp.concatenate([top, h_ref[pl.ds(r0, TM_FF), :], bot], axis=0)

    def up_conv(col):
        p = _dot(slab, wup_ref[0, :, col:col + TF])
        cw = cw_ref[0, :, col:col + TF]
        h0 = FF_HALO
        return (p[h0 - 1:h0 - 1 + TM_FF] * cw[0:1] + p[h0:h0 + TM_FF] * cw[1:2]
                + p[h0 + 1:h0 + 1 + TM_FF] * cw[2:3] + cb_ref[0, :, col:col + TF])

    for j in range(D_FF // TF):
        a = up_conv(j * TF)
        b = up_conv(D_FF + j * TF)
        act_s[:, j * TF:(j + 1) * TF] = (_silu(a) * b).astype(BF)
    y = _dot(act_s[...], wdn_ref[0])
    o_ref[...] = x1_ref[...] + gt_ref[0] * _rms(y, nw_ref[0])


def _ffn(l, h2, x1, w_up, conv_w, conv_b, w_down, mod3, nw):
    nblk = S // TM_FF
    row = pl.BlockSpec((TM_FF, D), lambda b, i: (b * nblk + i, 0))
    return pl.pallas_call(
        _ffn_kernel,
        grid=(B, nblk),
        in_specs=[pl.BlockSpec((S, D), lambda b, i: (b, 0)), row,
                  _layer_spec(l, (D, 2 * D_FF)), _layer_spec(l, (3, 2 * D_FF)), _layer_spec(l, (1, 2 * D_FF)),
                  _layer_spec(l, (D_FF, D)), _norm_spec(l, 3),
                  pl.BlockSpec((1, 1, D), lambda b, i: (l * B * 6 + b * 6 + 5, 0, 0))],
        out_specs=row,
        out_shape=jax.ShapeDtypeStruct((T, D), F32),
        scratch_shapes=[pltpu.VMEM((TM_FF, D_FF), BF)],
        compiler_params=_cparams(("parallel", "arbitrary")),
        name="conv_ffn",
    )(h2, x1, w_up, conv_w, conv_b, w_down, nw, mod3)


def _pair_lanes(t):
    t = t.reshape(t.shape[:-1] + (2, DN_H // DN_HP, DN_HP))
    t = jnp.moveaxis(t, -2, -3)
    return t.reshape(t.shape[:-2] + (2 * DN_HP,))


def _prep_w_in(w_in):
    a_uv, qkv, z, beta, alpha, cq, ckv, kr, gate = jnp.split(
        w_in, [1024, 2560, 3072, 3080, 3088, 3472, 3728, 3792], axis=-1)
    kr_sw = jnp.concatenate([kr[..., 32:], kr[..., :32]], axis=-1)
    bp = _pair_lanes(beta)
    ap = _pair_lanes(alpha)
    pad = jnp.zeros(w_in.shape[:-1] + (DN_H // DN_HP, 120), w_in.dtype)
    ba = jnp.concatenate([bp, ap, pad], axis=-1).reshape(w_in.shape[:-1] + (256,))
    return jnp.concatenate([a_uv, qkv, z, cq, ckv, kr, kr_sw, ba, gate], axis=-1).astype(BF)


def _prep_gpar(a_log, dt_bias):
    def lanes(t):
        tp = _pair_lanes(t.reshape(DEPTH, 2 * DN_H))
        z4 = jnp.zeros((DEPTH, DN_H // DN_HP, 4), F32)
        z120 = jnp.zeros((DEPTH, DN_H // DN_HP, 120), F32)
        return jnp.concatenate([z4, tp, z120], axis=-1)
    return jnp.stack([lanes(a_log), lanes(dt_bias)], axis=2)


def _prep_w_uq(w_uq):
    wq = w_uq.reshape(DEPTH, ML_QR, ML_H, ML_DQ)
    nope = wq[..., :ML_NOPE].reshape(DEPTH, ML_QR, ML_H * ML_NOPE)
    pe = wq[..., ML_NOPE:]
    pe_sw = jnp.concatenate([pe[..., 32:], pe[..., :32]], axis=-1)
    return jnp.concatenate([nope, pe.reshape(DEPTH, ML_QR, 256), pe_sw.reshape(DEPTH, ML_QR, 256)],
                           axis=-1).astype(BF)


def _prep_w_ukv(w_ukv):
    wk = w_ukv.reshape(DEPTH, ML_KVR, ML_H, ML_NOPE + ML_V)
    return jnp.concatenate([wk[..., :ML_NOPE].reshape(DEPTH, ML_KVR, 512),
                            wk[..., ML_NOPE:].reshape(DEPTH, ML_KVR, 512)], axis=-1).astype(BF)


@jax.jit
def _forward(x, c, positions, w_ada, b_ada, norm_w, w_in, b_gate, a_ln_w, a_ln_b, a_w_sp, a_b_sp, dn_conv_w,
             dn_a_log, dn_dt_bias, dn_o_norm_w, mla_q_norm_w, mla_kv_norm_w, mla_w_uq, mla_w_ukv, w_branch,
             w_o, ffn_w_up, ffn_conv_w, ffn_conv_b, ffn_w_down):
    mod = _modulation(c, w_ada, b_ada)
    cc, ss = _rope_tables(positions)
    w_in_r = _prep_w_in(w_in)
    wuq_r = _prep_w_uq(mla_w_uq)
    wukv_r = _prep_w_ukv(mla_w_ukv)
    wsp = a_w_sp.astype(BF)
    bsp = jnp.broadcast_to(a_b_sp[..., None], (DEPTH, 4, 128, 128))
    wbr = w_branch.astype(BF)
    wo = w_o.astype(BF)
    wup = ffn_w_up.astype(BF)
    wdn = ffn_w_down.astype(BF)
    gpar = _prep_gpar(dn_a_log, dn_dt_bias)
    mod3 = mod.reshape(DEPTH * B * 6, 1, D)
    lnw = a_ln_w.reshape(DEPTH, 1, A_W)
    lnb = a_ln_b.reshape(DEPTH, 1, A_W)
    qnw = mla_q_norm_w.reshape(DEPTH, 1, ML_QR)
    kvnw = mla_kv_norm_w.reshape(DEPTH, 1, ML_KVR)
    onw = dn_o_norm_w.reshape(DEPTH, 1, DN_D)
    cb = ffn_conv_b.reshape(DEPTH, 1, 2 * D_FF)
    nw = norm_w.reshape(DEPTH * 4, 1, D)

    xf = x.reshape(T, D)
    for l in range(DEPTH):
        oa, qkv, z, ba, gate, q, k, v = _in_proj(l, xf, mod3, nw, w_in_r, lnw, lnb, wsp, bsp, qnw, kvnw,
                                                 wuq_r, wukv_r, cc, ss, dn_conv_w)
        ob = _deltanet(l, qkv, z, ba, gpar, onw)
        oc = _attention(q, k, v)
        x1, h2 = _merge(l, oa, ob, oc, gate, b_gate, wbr, wo, xf, mod3, nw)
        xf = _ffn(l, h2, x1, wup, ffn_conv_w, cb, wdn, mod3, nw)
    return xf.reshape(B, S, D)


def kernel(x, c, positions, w_ada, b_ada, norm_w, w_in, b_gate, a_ln_w, a_ln_b, a_w_sp, a_b_sp, dn_conv_w,
           dn_a_log, dn_dt_bias, dn_o_norm_w, mla_q_norm_w, mla_kv_norm_w, mla_w_uq, mla_w_ukv, w_branch, w_o,
           ffn_w_up, ffn_conv_w, ffn_conv_b, ffn_w_down):
    return _forward(x, c, positions, w_ada, b_ada, norm_w, w_in, b_gate, a_ln_w, a_ln_b, a_w_sp, a_b_sp,
                    dn_conv_w, dn_a_log, dn_dt_bias, dn_o_norm_w, mla_q_norm_w, mla_kv_norm_w, mla_w_uq,
                    mla_w_ukv, w_branch, w_o, ffn_w_up, ffn_conv_w, ffn_conv_b, ffn_w_down)
```

```python
import math

import jax
import jax.numpy as jnp
from jax import lax
from jax.experimental import pallas as pl
from jax.experimental.pallas import tpu as pltpu

F32 = jnp.float32
BF = jnp.bfloat16

D = 1024
B = 8
S = 2048
DEPTH = 4
T = B * S

A_W = 512
A_CHUNK = 128
DN_H = 4
DN_D = 128
DN_W = 512
DN_CONV = 5
DN_C = 128
DN_NC = S // DN_C
DN_HP = 2
DN_UNROLL = 4
ML_H = 4
ML_QR = 384
ML_KVR = 256
ML_NOPE = 128
ML_ROPE = 64
ML_V = 128
ML_DQ = ML_NOPE + ML_ROPE
ROPE_THETA = 10000.0
D_FF = 2816
RMS_EPS = 1e-6
LN_EPS = 1e-5

C_A = 0
C_QKV = 1024
C_Z = 2560
C_CQ = 3072
C_CKV = 3456
C_KR = 3712
C_BA = 3840
C_GATE = 4096
NP_IN = 7168

VMEM_LIMIT = 56 * 1024 * 1024


def _cparams(sem):
    return pltpu.CompilerParams(dimension_semantics=sem, vmem_limit_bytes=VMEM_LIMIT)


def _dot(a, b):
    return jnp.dot(a, b, preferred_element_type=F32)


def _dot_nt(a, b):
    return lax.dot_general(a, b, (((1,), (1,)), ((), ())), preferred_element_type=F32)


def _rms(x, w, eps=RMS_EPS):
    return x * lax.rsqrt(jnp.mean(x * x, axis=-1, keepdims=True) + eps) * w


def _sigmoid(x):
    return 1.0 / (1.0 + jnp.exp(-x))


def _silu(x):
    return x * _sigmoid(x)


def _gelu_tanh(x):
    c = math.sqrt(2.0 / math.pi)
    return 0.5 * x * (1.0 + jnp.tanh(c * (x + 0.044715 * (x * x * x))))


def _layer_spec(l, tail, idx=None):
    idx = (0,) * len(tail) if idx is None else idx
    return pl.BlockSpec((1,) + tuple(tail), lambda *_: (l,) + tuple(idx), pipeline_mode=pl.Buffered(1))


def _norm_spec(l, k):
    return pl.BlockSpec((1, 1, D), lambda *_: (l * 4 + k, 0, 0), pipeline_mode=pl.Buffered(1))


def _mod_kernel(c_ref, w_ref, b_ref, o_ref):
    c = c_ref[...]
    ca = _silu(c).astype(BF)
    o_ref[0] = _dot(ca, w_ref[0].astype(BF)) + b_ref[0]


def _modulation(c, w_ada, b_ada):
    tn = 1536
    return pl.pallas_call(
        _mod_kernel,
        grid=(DEPTH, 6 * D // tn),
        in_specs=[pl.BlockSpec((B, D), lambda l, j: (0, 0)),
                  pl.BlockSpec((1, D, tn), lambda l, j: (l, 0, j)),
                  pl.BlockSpec((1, 1, tn), lambda l, j: (l, 0, j))],
        out_specs=pl.BlockSpec((1, B, tn), lambda l, j: (l, 0, j)),
        out_shape=jax.ShapeDtypeStruct((DEPTH, B, 6 * D), F32),
        compiler_params=_cparams(("parallel", "parallel")),
        name="adaln_mod",
    )(c, w_ada, b_ada.reshape(DEPTH, 1, 6 * D))


def _rope_kernel(pos_ref, invf_ref, sign_ref, cc_ref, ss_ref):
    ang = pos_ref[...].astype(F32) * invf_ref[...]
    c = jnp.cos(ang)
    s = jnp.sin(ang) * sign_ref[...]
    cc_ref[...] = jnp.concatenate([c, c], axis=-1)
    ss_ref[...] = jnp.concatenate([s, s], axis=-1)


def _rope_tables(positions):
    half = ML_ROPE // 2
    inv_freq = ROPE_THETA ** (-jnp.arange(0, ML_ROPE, 2, dtype=F32) / ML_ROPE)
    invf = jnp.tile(inv_freq, 4).reshape(1, 128)
    sign = jnp.tile(jnp.concatenate([-jnp.ones((half,), F32), jnp.ones((half,), F32)]), 2).reshape(1, 128)
    return pl.pallas_call(
        _rope_kernel,
        grid=(B,),
        in_specs=[pl.BlockSpec((S, 1), lambda b: (b, 0)),
                  pl.BlockSpec((1, 128), lambda b: (0, 0)),
                  pl.BlockSpec((1, 128), lambda b: (0, 0))],
        out_specs=[pl.BlockSpec((S, 256), lambda b: (b, 0)),
                   pl.BlockSpec((S, 256), lambda b: (b, 0))],
        out_shape=[jax.ShapeDtypeStruct((T, 256), F32), jax.ShapeDtypeStruct((T, 256), F32)],
        compiler_params=_cparams(("parallel",)),
        name="rope_tables",
    )(positions.reshape(T, 1), invf, sign)


TM_IN = 512
IN_HALO = 16


def _in_kernel(x_ref, xt_ref, xb_ref, nw_ref, sc_ref, sh_ref, w_ref, lnw_ref, lnb_ref, wsp_ref, bsp_ref,
               qnw_ref, kvnw_ref, wuq_ref, wukv_ref, cc_ref, ss_ref, dcw_ref,
               oa_ref, qkv_ref, z_ref, ba_ref, gate_ref, q_ref, k_ref, v_ref):
    i = pl.program_id(0)
    per_b = S // TM_IN

    def modulate(xx):
        return (_rms(xx, nw_ref[0]) * (1.0 + sc_ref[0]) + sh_ref[0]).astype(BF)

    hb = modulate(x_ref[...])

    def seg(c0, n):
        return _dot(hb, w_ref[0, :, c0:c0 + n])

    zero = jnp.zeros((IN_HALO, D), BF)
    top = jnp.where(i % per_b > 0, modulate(xt_ref[...]), zero)
    bot = jnp.where(i % per_b < per_b - 1, modulate(xb_ref[...]), zero)
    slab = jnp.concatenate([top, hb, bot], axis=0)
    def qkv_mm(part):
        c0 = part * DN_W
        return _dot(slab, w_ref[0, :, C_QKV + c0:C_QKV + c0 + DN_W])

    def qkv_finish(part, p):
        c0 = part * DN_W
        acc = None
        for t in range(DN_CONV):
            o0 = IN_HALO + t - DN_CONV // 2
            term = p[o0:o0 + TM_IN] * dcw_ref[0, t:t + 1, c0:c0 + DN_W]
            acc = term if acc is None else acc + term
        y = _silu(acc)
        for hh in range(DN_H):
            l = slice(hh * DN_D, (hh + 1) * DN_D)
            yh = y[:, l]
            if part == 0:
                yh = yh * (lax.rsqrt(jnp.sum(yh * yh, axis=-1, keepdims=True) + 1e-6) * (DN_D ** -0.5))
            elif part == 1:
                yh = yh * lax.rsqrt(jnp.sum(yh * yh, axis=-1, keepdims=True) + 1e-6)
            qkv_ref[:, c0 + hh * DN_D:c0 + (hh + 1) * DN_D] = yh.astype(BF)

    def gate_mm(n):
        return seg(C_GATE + n * D, D)

    def gate_store(n, gt):
        gate_ref[:, n * D:(n + 1) * D] = gt.astype(BF)

    p_q = qkv_mm(0)
    p_k = qkv_mm(1)
    qkv_finish(0, p_q)
    p_v = qkv_mm(2)
    qkv_finish(1, p_k)
    a_raw = seg(C_A, 2 * A_W)
    qkv_finish(2, p_v)
    g0 = gate_mm(0)

    a = _gelu_tanh(a_raw)
    u = a[:, :A_W]
    v = a[:, A_W:]
    mu = jnp.mean(v, axis=-1, keepdims=True)
    vc = v - mu
    var = jnp.mean(vc * vc, axis=-1, keepdims=True)
    vn = (vc * lax.rsqrt(var + LN_EPS) * lnw_ref[0] + lnb_ref[0]).astype(BF)
    lowrank = seg(C_CQ, ML_QR + ML_KVR + 128)
    gate_store(0, g0)
    for c in range(TM_IN // A_CHUNK):
        r = slice(c * A_CHUNK, (c + 1) * A_CHUNK)
        for g in range(4):
            l = slice(g * 128, (g + 1) * 128)
            mixed = _dot(wsp_ref[0, g], vn[r, l]) + bsp_ref[0, g]
            oa_ref[r, l] = (u[r, l] * mixed).astype(BF)

    g1 = gate_mm(1)

    cc = cc_ref[...]
    ss = ss_ref[...]
    cqn = _rms(lowrank[:, :ML_QR], qnw_ref[0]).astype(BF)
    ckvn = _rms(lowrank[:, ML_QR:ML_QR + ML_KVR], kvnw_ref[0]).astype(BF)
    krr = lowrank[:, ML_QR + ML_KVR:]
    qall = _dot(cqn, wuq_ref[0]) * (ML_DQ ** -0.5)
    kvall = _dot(ckvn, wukv_ref[0])
    gate_store(1, g1)
    g2 = gate_mm(2)
    q_rot = qall[:, 512:768] * cc + qall[:, 768:1024] * ss
    k_pe = (krr[:, :64] * cc[:, :64] + krr[:, 64:] * ss[:, :64]).astype(BF)
    for hh in range(ML_H):
        l = slice(hh * 128, (hh + 1) * 128)
        q_ref[0, hh, :, 0:ML_NOPE] = qall[:, l].astype(BF)
        q_ref[0, hh, :, ML_NOPE:ML_DQ] = q_rot[:, hh * 64:(hh + 1) * 64].astype(BF)
        k_ref[0, hh, :, 0:ML_NOPE] = kvall[:, l].astype(BF)
        k_ref[0, hh, :, ML_NOPE:ML_DQ] = k_pe
        v_ref[0, hh] = kvall[:, 512 + hh * 128:512 + (hh + 1) * 128].astype(BF)
    zb = seg(C_Z, DN_W)
    bab = seg(C_BA, 256)
    gate_store(2, g2)
    z_ref[...] = zb.astype(BF)
    ba_ref[...] = bab


def _in_proj(l, x, mod3, nw, w_in, lnw, lnb, wsp, bsp, qnw, kvnw, wuq, wukv, cc, ss, dcw):
    tm = TM_IN
    per_b = S // tm
    nh = tm // IN_HALO
    row = lambda n: pl.BlockSpec((tm, n), lambda i: (i, 0))
    modspec = lambda k: pl.BlockSpec((1, 1, D), lambda i: (l * B * 6 + (i // per_b) * 6 + k, 0, 0))
    hspec = lambda n: pl.BlockSpec((1, ML_H, tm, n), lambda i: (i // per_b, 0, i % per_b, 0))
    return pl.pallas_call(
        _in_kernel,
        grid=(T // tm,),
        in_specs=[row(D),
                  pl.BlockSpec((IN_HALO, D), lambda i: (jnp.maximum(i * nh - 1, 0), 0)),
                  pl.BlockSpec((IN_HALO, D), lambda i: (jnp.minimum((i + 1) * nh, T // IN_HALO - 1), 0)),
                  _norm_spec(l, 0), modspec(1), modspec(0), _layer_spec(l, (D, NP_IN)),
                  _layer_spec(l, (1, A_W)), _layer_spec(l, (1, A_W)), _layer_spec(l, (4, 128, 128)),
                  _layer_spec(l, (4, 128, 128)), _layer_spec(l, (1, ML_QR)), _layer_spec(l, (1, ML_KVR)),
                  _layer_spec(l, (ML_QR, 1024)), _layer_spec(l, (ML_KVR, 1024)), row(256), row(256),
                  _layer_spec(l, (DN_CONV, 3 * DN_W))],
        out_specs=[row(A_W), row(3 * DN_W), row(DN_W), row(256), row(3 * D),
                   hspec(ML_DQ), hspec(ML_DQ), hspec(ML_V)],
        out_shape=[jax.ShapeDtypeStruct((T, A_W), BF), jax.ShapeDtypeStruct((T, 3 * DN_W), BF),
                   jax.ShapeDtypeStruct((T, DN_W), BF), jax.ShapeDtypeStruct((T, 256), F32),
                   jax.ShapeDtypeStruct((T, 3 * D), BF),
                   jax.ShapeDtypeStruct((B, ML_H, S, ML_DQ), BF),
                   jax.ShapeDtypeStruct((B, ML_H, S, ML_DQ), BF),
                   jax.ShapeDtypeStruct((B, ML_H, S, ML_V), BF)],
        compiler_params=_cparams(("parallel",)),
        name="in_proj",
    )(x, x, x, nw, mod3, mod3, w_in, lnw, lnb, wsp, bsp, qnw, kvnw, wuq, wukv, cc, ss, dcw)


TQ = 2048
TQ_SUB = 256


def _attn_kernel(q_ref, k_ref, v_ref, o_ref):
    k = k_ref[0, 0]
    v = v_ref[0, 0]
    nsub = TQ // TQ_SUB

    def scores(i):
        return _dot_nt(q_ref[0, 0, i * TQ_SUB:(i + 1) * TQ_SUB, :], k)

    s_next = scores(0)
    for i in range(nsub):
        s = s_next
        if i + 1 < nsub:
            s_next = scores(i + 1)
        m = jnp.max(s, axis=-1, keepdims=True)
        p = jnp.exp(s - m)
        l = jnp.sum(p, axis=-1, keepdims=True)
        o = _dot(p.astype(BF), v)
        o_ref[i * TQ_SUB:(i + 1) * TQ_SUB, :] = (o / l).astype(BF)


def _attention(q, k, v):
    nq = S // TQ
    return pl.pallas_call(
        _attn_kernel,
        grid=(B, ML_H, nq),
        in_specs=[pl.BlockSpec((1, 1, TQ, ML_DQ), lambda b, h, i: (b, h, i, 0)),
                  pl.BlockSpec((1, 1, S, ML_DQ), lambda b, h, i: (b, h, 0, 0)),
                  pl.BlockSpec((1, 1, S, ML_V), lambda b, h, i: (b, h, 0, 0))],
        out_specs=pl.BlockSpec((TQ, ML_V), lambda b, h, i: (b * nq + i, h)),
        out_shape=jax.ShapeDtypeStruct((T, ML_H * ML_V), BF),
        compiler_params=_cparams(("parallel", "parallel", "parallel")),
        name="mla_attn",
    )(q, k, v)


def _split3(x):
    hi = x.astype(BF)
    r = x - hi.astype(F32)
    mid = r.astype(BF)
    lo = (r - mid.astype(F32)).astype(BF)
    return hi, mid, lo


def _dn_kernel(q_ref, k_ref, v_ref, z_ref, ba_ref, gpar_ref, onw_ref, o_ref,
               gb_s, dlt_s, lev_s, u_s, wq_s, atk_s, gl_s, os_s, st_s):
    c = DN_C
    nc = DN_NC
    wp = DN_HP * DN_D
    ba = ba_ref[...]
    lane = lax.broadcasted_iota(jnp.int32, (1, 128), 1)
    xg = ba + gpar_ref[0, 0, 1:2, :]
    sp = jnp.maximum(xg, 0.0) + jnp.log1p(jnp.exp(-jnp.abs(xg)))
    gb_s[...] = jnp.where(lane < 4, _sigmoid(ba), -jnp.exp(gpar_ref[0, 0, 0:1, :]) * sp)

    ri = lax.broadcasted_iota(jnp.int32, (c, c), 0)
    ci = lax.broadcasted_iota(jnp.int32, (c, c), 1)
    dlt_s[...] = ri - ci
    xr = jnp.bitwise_xor(ri, ci)
    lev = jnp.zeros((c, c), jnp.int32)
    bs = 1
    while bs < c:
        lev = lev + (xr >= bs).astype(jnp.int32)
        bs *= 2
    lev_s[...] = lev
    n_lev = int(math.log2(c))

    def bdiag(a, b):
        za = jnp.zeros_like(a)
        return jnp.concatenate([jnp.concatenate([a, za], axis=1), jnp.concatenate([za, b], axis=1)], axis=0)

    def prep(i, carry):
        lv = lev_s[...]
        dlt = dlt_s[...]
        lv2 = jnp.concatenate([lv, lv], axis=1)
        incl = (dlt >= 0, dlt <= 0)
        strict = (dlt > 0, dlt < 0)
        tri = tuple(jnp.where(m, 1.0, 0.0).astype(BF) for m in incl)
        eye = jnp.where(lv == 0, 1.0, 0.0).astype(BF)

        cells = []
        for j in range(DN_UNROLL):
            n = i * DN_UNROLL + j
            r0 = pl.multiple_of(n * c, c)
            qpb = q_ref[pl.ds(r0, c), :]
            kpb = k_ref[pl.ds(r0, c), :]
            kp = kpb.astype(F32)
            gbc = gb_s[pl.ds(r0, c), :]
            pair = []
            for hh in range(DN_HP):
                l = slice(hh * DN_D, (hh + 1) * DN_D)
                kc = kp[:, l]
                beta = [jnp.broadcast_to(gbc[:, 2 * d + hh:2 * d + hh + 1], (c, c)) for d in range(2)]
                g = [jnp.broadcast_to(gbc[:, 4 + 2 * d + hh:5 + 2 * d + hh], (c, c)) for d in range(2)]
                kb = [kc * beta[d] for d in range(2)]
                pair.append(dict(n=n, r0=r0, qc=qpb[:, l].astype(F32), kc=kc,
                                 vc=v_ref[pl.ds(r0, c), l].astype(F32), beta=beta, g=g, kb=kb))
            lhs = jnp.concatenate(
                [qpb] + [jnp.concatenate([ce["kb"][d].astype(BF) for ce in pair], axis=1) for d in range(2)], axis=0)
            prod = _dot_nt(lhs, bdiag(kpb[:, :DN_D], kpb[:, DN_D:]))
            for hh, ce in enumerate(pair):
                ce["prod"] = prod[:, hh * c:(hh + 1) * c]
                cells.append(ce)

        cum = []
        for d in range(2):
            rg = jnp.concatenate([jnp.where(strict[d], ce["g"][d], 0.0) for ce in cells], axis=1)
            cum.append(sum(_dot(tri[d], part) for part in _split3(rg)))

        items = []
        for j in range(DN_UNROLL):
            for d in range(2):
                low, rhs, at, qdec, kdec, gl = [], [], [], [], [], []
                for hh in range(DN_HP):
                    ce = cells[j * DN_HP + hh]
                    diff = cum[d][:, (j * DN_HP + hh) * c:(j * DN_HP + hh + 1) * c]
                    e0 = 0 if d == 0 else c - 1
                    gc_b = jnp.broadcast_to(diff[:, e0:e0 + 1], (c, c)) + ce["g"][d][e0:e0 + 1, :]
                    decay = jnp.where(incl[d], jnp.exp(jnp.where(incl[d], diff, 0.0)), 0.0)
                    glast = gc_b[c - 1:c, :] if d == 0 else gc_b[0:1, :]
                    e_gc = jnp.exp(gc_b)
                    low.append(jnp.where(strict[d], ce["prod"][(1 + d) * c:(2 + d) * c] * decay, 0.0))
                    at.append(jnp.where(incl[d], ce["prod"][0:c] * decay, 0.0).astype(BF))
                    qdec.append((ce["qc"] * e_gc).astype(BF))
                    kdec.append((ce["kc"] * jnp.exp(glast - gc_b)).astype(BF))
                    rhs.append(jnp.concatenate([ce["vc"] * ce["beta"][d], ce["kb"][d] * e_gc], axis=1).astype(BF))
                    gl.append(jnp.exp(glast))
                lowp = jnp.concatenate(low, axis=1)
                n = cells[j * DN_HP]["n"]
                r0 = cells[j * DN_HP]["r0"]
                wq_s[d, n, c:2 * c, :] = jnp.concatenate(qdec, axis=1)
                atk_s[d, n, 0:c, :] = jnp.concatenate(at, axis=1)
                atk_s[d, n, c:2 * c, :] = _dot_nt(eye, jnp.concatenate(kdec, axis=0)).astype(BF)
                gl_s[d, n] = jnp.broadcast_to(jnp.concatenate(gl, axis=1), (8, wp))
                x = jnp.where(lv2 == 0, 1.0, 0.0) - jnp.where(lv2 == 1, lowp, 0.0)
                items.append(dict(d=d, n=n, r0=r0, lowp=lowp, rhs=rhs, x=x))

        for k in range(2, n_lev + 1):
            xb = [it["x"].astype(BF) for it in items]
            t1 = []
            for it, b in zip(items, xb):
                e = jnp.where(lv2 == k, it["lowp"], 0.0).astype(BF)
                t1.append(_dot(b, bdiag(e[:, :c], e[:, c:])).astype(BF))
            t2 = [_dot(t, bdiag(b[:, :c], b[:, c:])) for t, b in zip(t1, xb)]
            for it, t in zip(items, t2):
                it["x"] = it["x"] - t

        sols = [_dot(it["x"].astype(BF), bdiag(it["rhs"][0], it["rhs"][1])) for it in items]
        for it, sol in zip(items, sols):
            d, n, r0 = it["d"], it["n"], it["r0"]
            u_s[d, pl.ds(r0, c), :] = jnp.concatenate([sol[:, 0:DN_D], sol[:, 2 * DN_D:3 * DN_D]], axis=1)
            wq_s[d, n, 0:c, :] = jnp.concatenate([sol[:, DN_D:2 * DN_D], sol[:, 3 * DN_D:]], axis=1).astype(BF)
        return carry

    lax.fori_loop(0, nc // DN_UNROLL, prep, 0)

    st_s[...] = jnp.zeros((2, DN_D, wp), F32)

    def scan(i, carry):
        ns = (i, nc - 1 - i)
        sts = [st_s[d] for d in range(2)]
        prs = []
        for d in range(2):
            sb = sts[d].astype(BF)
            prs.append(_dot(wq_s[d, ns[d]], bdiag(sb[:, :DN_D], sb[:, DN_D:])))
        r2s = []
        for d in range(2):
            r0 = pl.multiple_of(ns[d] * c, c)
            vn = (u_s[d, pl.ds(r0, c), :] - prs[d][:c]).astype(BF)
            r2s.append(_dot(atk_s[d, ns[d]], bdiag(vn[:, :DN_D], vn[:, DN_D:])))
        for d in range(2):
            r0 = pl.multiple_of(ns[d] * c, c)
            os_s[d, pl.ds(r0, c), :] = prs[d][c:] + r2s[d][:c]
            st_s[d] = sts[d] * gl_s[d, ns[d]][0:1, :] + r2s[d][c:]
        return carry

    lax.fori_loop(0, nc, scan, 0)

    o = os_s[0] + os_s[1]
    sil = _silu(z_ref[...].astype(F32))
    for hh in range(DN_HP):
        l = slice(hh * DN_D, (hh + 1) * DN_D)
        o_ref[:, l] = (_rms(o[:, l], onw_ref[0]) * sil[:, l]).astype(BF)


def _deltanet(l, qkv, z, ba, gpar, onw):
    c, nc, wp = DN_C, DN_NC, DN_HP * DN_D
    npair = DN_H // DN_HP
    col = lambda off: pl.BlockSpec((S, wp), lambda b, p: (b, off + p))
    return pl.pallas_call(
        _dn_kernel,
        grid=(B, npair),
        in_specs=[col(0), col(npair), col(2 * npair), col(0),
                  pl.BlockSpec((S, 128), lambda b, p: (b, p)),
                  pl.BlockSpec((1, 1, 2, 128), lambda b, p: (l, p, 0, 0)), _layer_spec(l, (1, DN_D))],
        out_specs=col(0),
        out_shape=jax.ShapeDtypeStruct((T, DN_W), BF),
        scratch_shapes=[pltpu.VMEM((S, 128), F32),
                        pltpu.VMEM((c, c), jnp.int32), pltpu.VMEM((c, c), jnp.int32),
                        pltpu.VMEM((2, S, wp), F32),
                        pltpu.VMEM((2, nc, 2 * c, wp), BF), pltpu.VMEM((2, nc, 2 * c, wp), BF),
                        pltpu.VMEM((2, nc, 8, wp), F32),
                        pltpu.VMEM((2, S, wp), F32), pltpu.VMEM((2, DN_D, wp), F32)],
        compiler_params=_cparams(("parallel", "parallel")),
        name="deltanet",
    )(qkv, qkv, qkv, z, ba, gpar, onw)


TM_MG = 512


def _merge_kernel(oa_ref, ob_ref, oc_ref, gate_ref, bg_ref, wbr_ref, wo_ref, x_ref, nw1_ref, gt1_ref,
                  nw2_ref, sc2_ref, sh2_ref, x1_ref, h2_ref):
    branches = (oa_ref, ob_ref, oc_ref)
    halves = [slice(k * TM_MG // 2, (k + 1) * TM_MG // 2) for k in range(2)]

    def ups(r):
        return [_dot(o_ref[r, :], wbr_ref[0, n]) for n, o_ref in enumerate(branches)]

    def gated(r, up):
        acc = None
        for n in range(3):
            g = _sigmoid(gate_ref[r, n * D:(n + 1) * D].astype(F32) + bg_ref[0, n:n + 1, :])
            acc = g * up[n] if acc is None else acc + g * up[n]
        return acc.astype(BF)

    def finish(r, y):
        x1 = x_ref[r, :] + gt1_ref[0] * _rms(y, nw1_ref[0])
        x1_ref[r, :] = x1
        h2_ref[r, :] = (_rms(x1, nw2_ref[0]) * (1.0 + sc2_ref[0]) + sh2_ref[0]).astype(BF)

    up0 = ups(halves[0])
    up1 = ups(halves[1])
    y0 = _dot(gated(halves[0], up0), wo_ref[0])
    y1 = _dot(gated(halves[1], up1), wo_ref[0])
    finish(halves[0], y0)
    finish(halves[1], y1)


def _merge(l, oa, ob, oc, gate, bg, wbr, wo, x, mod3, nw):
    tm = TM_MG
    per_b = S // tm
    row = lambda n: pl.BlockSpec((tm, n), lambda i: (i, 0))
    modspec = lambda k: pl.BlockSpec((1, 1, D), lambda i: (l * B * 6 + (i // per_b) * 6 + k, 0, 0))
    return pl.pallas_call(
        _merge_kernel,
        grid=(T // tm,),
        in_specs=[row(512), row(512), row(512), row(3 * D), _layer_spec(l, (3, D)),
                  _layer_spec(l, (3, 512, D)), _layer_spec(l, (D, D)), row(D), _norm_spec(l, 1),
                  modspec(2), _norm_spec(l, 2), modspec(4), modspec(3)],
        out_specs=[row(D), row(D)],
        out_shape=[jax.ShapeDtypeStruct((T, D), F32), jax.ShapeDtypeStruct((T, D), BF)],
        compiler_params=_cparams(("parallel",)),
        name="merge",
    )(oa, ob, oc, gate, bg, wbr, wo, x, nw, mod3, nw, mod3, mod3)


TM_FF = 512
TF = 256
FF_HALO = 16


def _ffn_kernel(h_ref, x1_ref, wup_ref, cw_ref, cb_ref, wdn_ref, nw_ref, gt_ref, o_ref, act_s):
    i = pl.program_id(1)
    nblk = S // TM_FF
    r0 = pl.multiple_of(i * TM_FF, TM_FF)
    top0 = pl.multiple_of(jnp.maximum(r0 - FF_HALO, 0), FF_HALO)
    bot0 = pl.multiple_of(jnp.minimum(r0 + TM_FF, S - FF_HALO), FF_HALO)
    zero = jnp.zeros((FF_HALO, D), BF)
    top = jnp.where(i > 0, h_ref[pl.ds(top0, FF_HALO), :], zero)
    bot = jnp.where(i < nblk - 1, h_ref[pl.ds(bot0, FF_HALO), :], zero)
    slab = jnp.concatenate([top, h_ref[pl.ds(r0, TM_FF), :], bot], axis=0)

    def up(col):
        return _dot(slab, wup_ref[0, :, col:col + TF])

    def conv(p, col):
        cw = cw_ref[0, :, col:col + TF]
        h0 = FF_HALO
        return (p[h0 - 1:h0 - 1 + TM_FF] * cw[0:1] + p[h0:h0 + TM_FF] * cw[1:2]
                + p[h0 + 1:h0 + 1 + TM_FF] * cw[2:3] + cb_ref[0, :, col:col + TF])

    nf = D_FF // TF
    nxt = (up(0), up(D_FF))
    for j in range(nf):
        pa, pb = nxt
        if j + 1 < nf:
            nxt = (up((j + 1) * TF), up(D_FF + (j + 1) * TF))
        a = conv(pa, j * TF)
        b = conv(pb, D_FF + j * TF)
        act_s[:, j * TF:(j + 1) * TF] = (_silu(a) * b).astype(BF)
    y = _dot(act_s[...], wdn_ref[0])
    o_ref[...] = x1_ref[...] + gt_ref[0] * _rms(y, nw_ref[0])


def _ffn(l, h2, x1, w_up, conv_w, conv_b, w_down, mod3, nw):
    nblk = S // TM_FF
    row = pl.BlockSpec((TM_FF, D), lambda b, i: (b * nblk + i, 0))
    return pl.pallas_call(
        _ffn_kernel,
        grid=(B, nblk),
        in_specs=[pl.BlockSpec((S, D), lambda b, i: (b, 0)), row,
                  _layer_spec(l, (D, 2 * D_FF)), _layer_spec(l, (3, 2 * D_FF)), _layer_spec(l, (1, 2 * D_FF)),
                  _layer_spec(l, (D_FF, D)), _norm_spec(l, 3),
                  pl.BlockSpec((1, 1, D), lambda b, i: (l * B * 6 + b * 6 + 5, 0, 0))],
        out_specs=row,
        out_shape=jax.ShapeDtypeStruct((T, D), F32),
        scratch_shapes=[pltpu.VMEM((TM_FF, D_FF), BF)],
        compiler_params=_cparams(("parallel", "arbitrary")),
        name="conv_ffn",
    )(h2, x1, w_up, conv_w, conv_b, w_down, nw, mod3)


def _pair_lanes(t):
    t = t.reshape(t.shape[:-1] + (2, DN_H // DN_HP, DN_HP))
    t = jnp.moveaxis(t, -2, -3)
    return t.reshape(t.shape[:-2] + (2 * DN_HP,))


def _prep_w_in(w_in):
    a_uv, qkv, z, beta, alpha, cq, ckv, kr, gate = jnp.split(
        w_in, [1024, 2560, 3072, 3080, 3088, 3472, 3728, 3792], axis=-1)
    kr_sw = jnp.concatenate([kr[..., 32:], kr[..., :32]], axis=-1)
    bp = _pair_lanes(beta)
    ap = _pair_lanes(alpha)
    pad = jnp.zeros(w_in.shape[:-1] + (DN_H // DN_HP, 120), w_in.dtype)
    ba = jnp.concatenate([bp, ap, pad], axis=-1).reshape(w_in.shape[:-1] + (256,))
    return jnp.concatenate([a_uv, qkv, z, cq, ckv, kr, kr_sw, ba, gate], axis=-1).astype(BF)


def _prep_gpar(a_log, dt_bias):
    def lanes(t):
        tp = _pair_lanes(t.reshape(DEPTH, 2 * DN_H))
        z4 = jnp.zeros((DEPTH, DN_H // DN_HP, 4), F32)
        z120 = jnp.zeros((DEPTH, DN_H // DN_HP, 120), F32)
        return jnp.concatenate([z4, tp, z120], axis=-1)
    return jnp.stack([lanes(a_log), lanes(dt_bias)], axis=2)


def _prep_w_uq(w_uq):
    wq = w_uq.reshape(DEPTH, ML_QR, ML_H, ML_DQ)
    nope = wq[..., :ML_NOPE].reshape(DEPTH, ML_QR, ML_H * ML_NOPE)
    pe = wq[..., ML_NOPE:]
    pe_sw = jnp.concatenate([pe[..., 32:], pe[..., :32]], axis=-1)
    return jnp.concatenate([nope, pe.reshape(DEPTH, ML_QR, 256), pe_sw.reshape(DEPTH, ML_QR, 256)],
                           axis=-1).astype(BF)


def _prep_w_ukv(w_ukv):
    wk = w_ukv.reshape(DEPTH, ML_KVR, ML_H, ML_NOPE + ML_V)
    return jnp.concatenate([wk[..., :ML_NOPE].reshape(DEPTH, ML_KVR, 512),
                            wk[..., ML_NOPE:].reshape(DEPTH, ML_KVR, 512)], axis=-1).astype(BF)


@jax.jit
def _forward(x, c, positions, w_ada, b_ada, norm_w, w_in, b_gate, a_ln_w, a_ln_b, a_w_sp, a_b_sp, dn_conv_w,
             dn_a_log, dn_dt_bias, dn_o_norm_w, mla_q_norm_w, mla_kv_norm_w, mla_w_uq, mla_w_ukv, w_branch,
             w_o, ffn_w_up, ffn_conv_w, ffn_conv_b, ffn_w_down):
    mod = _modulation(c, w_ada, b_ada)
    cc, ss = _rope_tables(positions)
    w_in_r = _prep_w_in(w_in)
    wuq_r = _prep_w_uq(mla_w_uq)
    wukv_r = _prep_w_ukv(mla_w_ukv)
    wsp = a_w_sp.astype(BF)
    bsp = jnp.broadcast_to(a_b_sp[..., None], (DEPTH, 4, 128, 128))
    wbr = w_branch.astype(BF)
    wo = w_o.astype(BF)
    wup = ffn_w_up.astype(BF)
    wdn = ffn_w_down.astype(BF)
    gpar = _prep_gpar(dn_a_log, dn_dt_bias)
    mod3 = mod.reshape(DEPTH * B * 6, 1, D)
    lnw = a_ln_w.reshape(DEPTH, 1, A_W)
    lnb = a_ln_b.reshape(DEPTH, 1, A_W)
    qnw = mla_q_norm_w.reshape(DEPTH, 1, ML_QR)
    kvnw = mla_kv_norm_w.reshape(DEPTH, 1, ML_KVR)
    onw = dn_o_norm_w.reshape(DEPTH, 1, DN_D)
    cb = ffn_conv_b.reshape(DEPTH, 1, 2 * D_FF)
    nw = norm_w.reshape(DEPTH * 4, 1, D)

    xf = x.reshape(T, D)
    for l in range(DEPTH):
        oa, qkv, z, ba, gate, q, k, v = _in_proj(l, xf, mod3, nw, w_in_r, lnw, lnb, wsp, bsp, qnw, kvnw,
                                                 wuq_r, wukv_r, cc, ss, dn_conv_w)
        ob = _deltanet(l, qkv, z, ba, gpar, onw)
        oc = _attention(q, k, v)
        x1, h2 = _merge(l, oa, ob, oc, gate, b_gate, wbr, wo, xf, mod3, nw)
        xf = _ffn(l, h2, x1, wup, ffn_conv_w, cb, wdn, mod3, nw)
    return xf.reshape(B, S, D)


def kernel(x, c, positions, w_ada, b_ada, norm_w, w_in, b_gate, a_ln_w, a_ln_b, a_w_sp, a_b_sp, dn_conv_w,
           dn_a_log, dn_dt_bias, dn_o_norm_w, mla_q_norm_w, mla_kv_norm_w, mla_w_uq, mla_w_ukv, w_branch, w_o,
           ffn_w_up, ffn_conv_w, ffn_conv_b, ffn_w_down):
    return _forward(x, c, positions, w_ada, b_ada, norm_w, w_in, b_gate, a_ln_w, a_ln_b, a_w_sp, a_b_sp,
                    dn_conv_w, dn_a_log, dn_dt_bias, dn_o_norm_w, mla_q_norm_w, mla_kv_norm_w, mla_w_uq,
                    mla_w_ukv, w_branch, w_o, ffn_w_up, ffn_conv_w, ffn_conv_b, ffn_w_down)
```

```python
import math

import jax
import jax.numpy as jnp
from jax import lax
from jax.experimental import pallas as pl
from jax.experimental.pallas import tpu as pltpu

F32 = jnp.float32
BF = jnp.bfloat16

D = 1024
B = 8
S = 2048
DEPTH = 4
T = B * S

A_W = 512
A_CHUNK = 128
DN_H = 4
DN_D = 128
DN_W = 512
DN_CONV = 5
DN_C = 128
DN_NC = S // DN_C
DN_HP = 2
DN_UNROLL = 4
ML_H = 4
ML_QR = 384
ML_KVR = 256
ML_NOPE = 128
ML_ROPE = 64
ML_V = 128
ML_DQ = ML_NOPE + ML_ROPE
ROPE_THETA = 10000.0
D_FF = 2816
RMS_EPS = 1e-6
LN_EPS = 1e-5

C_A = 0
C_QKV = 1024
C_Z = 2560
C_CQ = 3072
C_CKV = 3456
C_KR = 3712
C_BA = 3840
C_GATE = 4096
NP_IN = 7168

VMEM_LIMIT = 56 * 1024 * 1024


def _cparams(sem):
    return pltpu.CompilerParams(dimension_semantics=sem, vmem_limit_bytes=VMEM_LIMIT)


def _dot(a, b):
    return jnp.dot(a, b, preferred_element_type=F32)


def _dot_nt(a, b):
    return lax.dot_general(a, b, (((1,), (1,)), ((), ())), preferred_element_type=F32)


def _rms(x, w, eps=RMS_EPS):
    return x * lax.rsqrt(jnp.mean(x * x, axis=-1, keepdims=True) + eps) * w


def _sigmoid(x):
    return 1.0 / (1.0 + jnp.exp(-x))


def _silu(x):
    return x * _sigmoid(x)


def _gelu_tanh(x):
    c = math.sqrt(2.0 / math.pi)
    return 0.5 * x * (1.0 + jnp.tanh(c * (x + 0.044715 * (x * x * x))))


def _layer_spec(l, tail, idx=None):
    idx = (0,) * len(tail) if idx is None else idx
    return pl.BlockSpec((1,) + tuple(tail), lambda *_: (l,) + tuple(idx), pipeline_mode=pl.Buffered(1))


def _norm_spec(l, k):
    return pl.BlockSpec((1, 1, D), lambda *_: (l * 4 + k, 0, 0), pipeline_mode=pl.Buffered(1))


def _mod_kernel(c_ref, w_ref, b_ref, o_ref):
    c = c_ref[...]
    ca = _silu(c).astype(BF)
    o_ref[0] = _dot(ca, w_ref[0].astype(BF)) + b_ref[0]


def _modulation(c, w_ada, b_ada):
    tn = 1536
    return pl.pallas_call(
        _mod_kernel,
        grid=(DEPTH, 6 * D // tn),
        in_specs=[pl.BlockSpec((B, D), lambda l, j: (0, 0)),
                  pl.BlockSpec((1, D, tn), lambda l, j: (l, 0, j)),
                  pl.BlockSpec((1, 1, tn), lambda l, j: (l, 0, j))],
        out_specs=pl.BlockSpec((1, B, tn), lambda l, j: (l, 0, j)),
        out_shape=jax.ShapeDtypeStruct((DEPTH, B, 6 * D), F32),
        compiler_params=_cparams(("parallel", "parallel")),
        name="adaln_mod",
    )(c, w_ada, b_ada.reshape(DEPTH, 1, 6 * D))


def _rope_kernel(pos_ref, invf_ref, sign_ref, cc_ref, ss_ref):
    ang = pos_ref[...].astype(F32) * invf_ref[...]
    c = jnp.cos(ang)
    s = jnp.sin(ang) * sign_ref[...]
    cc_ref[...] = jnp.concatenate([c, c], axis=-1)
    ss_ref[...] = jnp.concatenate([s, s], axis=-1)


def _rope_tables(positions):
    half = ML_ROPE // 2
    inv_freq = ROPE_THETA ** (-jnp.arange(0, ML_ROPE, 2, dtype=F32) / ML_ROPE)
    invf = jnp.tile(inv_freq, 4).reshape(1, 128)
    sign = jnp.tile(jnp.concatenate([-jnp.ones((half,), F32), jnp.ones((half,), F32)]), 2).reshape(1, 128)
    return pl.pallas_call(
        _rope_kernel,
        grid=(B,),
        in_specs=[pl.BlockSpec((S, 1), lambda b: (b, 0)),
                  pl.BlockSpec((1, 128), lambda b: (0, 0)),
                  pl.BlockSpec((1, 128), lambda b: (0, 0))],
        out_specs=[pl.BlockSpec((S, 256), lambda b: (b, 0)),
                   pl.BlockSpec((S, 256), lambda b: (b, 0))],
        out_shape=[jax.ShapeDtypeStruct((T, 256), F32), jax.ShapeDtypeStruct((T, 256), F32)],
        compiler_params=_cparams(("parallel",)),
        name="rope_tables",
    )(positions.reshape(T, 1), invf, sign)


TM_IN = 512
IN_HALO = 16


def _in_kernel(x_ref, xt_ref, xb_ref, nw_ref, sc_ref, sh_ref, w_ref, lnw_ref, lnb_ref, wsp_ref, bsp_ref,
               qnw_ref, kvnw_ref, wuq_ref, wukv_ref, cc_ref, ss_ref, dcw_ref,
               oa_ref, qkv_ref, z_ref, ba_ref, gate_ref, q_ref, k_ref, v_ref):
    i = pl.program_id(0)
    per_b = S // TM_IN

    def modulate(xx):
        return (_rms(xx, nw_ref[0]) * (1.0 + sc_ref[0]) + sh_ref[0]).astype(BF)

    hb = modulate(x_ref[...])

    def seg(c0, n):
        return _dot_nt(hb, w_ref[0, c0:c0 + n, :])

    zero = jnp.zeros((IN_HALO, D), BF)
    top = jnp.where(i % per_b > 0, modulate(xt_ref[...]), zero)
    bot = jnp.where(i % per_b < per_b - 1, modulate(xb_ref[...]), zero)
    slab = jnp.concatenate([top, hb, bot], axis=0)
    def qkv_mm(part):
        c0 = part * DN_W
        return _dot_nt(slab, w_ref[0, C_QKV + c0:C_QKV + c0 + DN_W, :])

    def qkv_finish(part, p):
        c0 = part * DN_W
        acc = None
        for t in range(DN_CONV):
            o0 = IN_HALO + t - DN_CONV // 2
            term = p[o0:o0 + TM_IN] * dcw_ref[0, t:t + 1, c0:c0 + DN_W]
            acc = term if acc is None else acc + term
        y = _silu(acc)
        for hh in range(DN_H):
            l = slice(hh * DN_D, (hh + 1) * DN_D)
            yh = y[:, l]
            if part == 0:
                yh = yh * (lax.rsqrt(jnp.sum(yh * yh, axis=-1, keepdims=True) + 1e-6) * (DN_D ** -0.5))
            elif part == 1:
                yh = yh * lax.rsqrt(jnp.sum(yh * yh, axis=-1, keepdims=True) + 1e-6)
            qkv_ref[:, c0 + hh * DN_D:c0 + (hh + 1) * DN_D] = yh.astype(BF)

    def gate_mm(n):
        return seg(C_GATE + n * D, D)

    def gate_store(n, gt):
        gate_ref[:, n * D:(n + 1) * D] = gt.astype(BF)

    p_q = qkv_mm(0)
    p_k = qkv_mm(1)
    qkv_finish(0, p_q)
    p_v = qkv_mm(2)
    qkv_finish(1, p_k)
    a_raw = seg(C_A, 2 * A_W)
    qkv_finish(2, p_v)
    g0 = gate_mm(0)

    a = _gelu_tanh(a_raw)
    u = a[:, :A_W]
    v = a[:, A_W:]
    mu = jnp.mean(v, axis=-1, keepdims=True)
    vc = v - mu
    var = jnp.mean(vc * vc, axis=-1, keepdims=True)
    vn = (vc * lax.rsqrt(var + LN_EPS) * lnw_ref[0] + lnb_ref[0]).astype(BF)
    lowrank = seg(C_CQ, ML_QR + ML_KVR + 128)
    gate_store(0, g0)
    for c in range(TM_IN // A_CHUNK):
        r = slice(c * A_CHUNK, (c + 1) * A_CHUNK)
        for g in range(4):
            l = slice(g * 128, (g + 1) * 128)
            mixed = _dot(wsp_ref[0, g], vn[r, l]) + bsp_ref[0, g]
            oa_ref[r, l] = (u[r, l] * mixed).astype(BF)

    g1 = gate_mm(1)

    cc = cc_ref[...]
    ss = ss_ref[...]
    cqn = _rms(lowrank[:, :ML_QR], qnw_ref[0]).astype(BF)
    ckvn = _rms(lowrank[:, ML_QR:ML_QR + ML_KVR], kvnw_ref[0]).astype(BF)
    krr = lowrank[:, ML_QR + ML_KVR:]
    qall = _dot(cqn, wuq_ref[0]) * (ML_DQ ** -0.5)
    kvall = _dot(ckvn, wukv_ref[0])
    gate_store(1, g1)
    g2 = gate_mm(2)
    q_rot = qall[:, 512:768] * cc + qall[:, 768:1024] * ss
    k_pe = (krr[:, :64] * cc[:, :64] + krr[:, 64:] * ss[:, :64]).astype(BF)
    for hh in range(ML_H):
        l = slice(hh * 128, (hh + 1) * 128)
        q_ref[0, hh, :, 0:ML_NOPE] = qall[:, l].astype(BF)
        q_ref[0, hh, :, ML_NOPE:ML_DQ] = q_rot[:, hh * 64:(hh + 1) * 64].astype(BF)
        k_ref[0, hh, :, 0:ML_NOPE] = kvall[:, l].astype(BF)
        k_ref[0, hh, :, ML_NOPE:ML_DQ] = k_pe
        v_ref[0, hh] = kvall[:, 512 + hh * 128:512 + (hh + 1) * 128].astype(BF)
    zb = seg(C_Z, DN_W)
    bab = seg(C_BA, 256)
    gate_store(2, g2)
    z_ref[...] = zb.astype(BF)
    ba_ref[...] = bab


def _in_proj(l, x, mod3, nw, w_in, lnw, lnb, wsp, bsp, qnw, kvnw, wuq, wukv, cc, ss, dcw):
    tm = TM_IN
    per_b = S // tm
    nh = tm // IN_HALO
    row = lambda n: pl.BlockSpec((tm, n), lambda i: (i, 0))
    modspec = lambda k: pl.BlockSpec((1, 1, D), lambda i: (l * B * 6 + (i // per_b) * 6 + k, 0, 0))
    hspec = lambda n: pl.BlockSpec((1, ML_H, tm, n), lambda i: (i // per_b, 0, i % per_b, 0))
    return pl.pallas_call(
        _in_kernel,
        grid=(T // tm,),
        in_specs=[row(D),
                  pl.BlockSpec((IN_HALO, D), lambda i: (jnp.maximum(i * nh - 1, 0), 0)),
                  pl.BlockSpec((IN_HALO, D), lambda i: (jnp.minimum((i + 1) * nh, T // IN_HALO - 1), 0)),
                  _norm_spec(l, 0), modspec(1), modspec(0), _layer_spec(l, (NP_IN, D)),
                  _layer_spec(l, (1, A_W)), _layer_spec(l, (1, A_W)), _layer_spec(l, (4, 128, 128)),
                  _layer_spec(l, (4, 128, 128)), _layer_spec(l, (1, ML_QR)), _layer_spec(l, (1, ML_KVR)),
                  _layer_spec(l, (ML_QR, 1024)), _layer_spec(l, (ML_KVR, 1024)), row(256), row(256),
                  _layer_spec(l, (DN_CONV, 3 * DN_W))],
        out_specs=[row(A_W), row(3 * DN_W), row(DN_W), row(256), row(3 * D),
                   hspec(ML_DQ), hspec(ML_DQ), hspec(ML_V)],
        out_shape=[jax.ShapeDtypeStruct((T, A_W), BF), jax.ShapeDtypeStruct((T, 3 * DN_W), BF),
                   jax.ShapeDtypeStruct((T, DN_W), BF), jax.ShapeDtypeStruct((T, 256), F32),
                   jax.ShapeDtypeStruct((T, 3 * D), BF),
                   jax.ShapeDtypeStruct((B, ML_H, S, ML_DQ), BF),
                   jax.ShapeDtypeStruct((B, ML_H, S, ML_DQ), BF),
                   jax.ShapeDtypeStruct((B, ML_H, S, ML_V), BF)],
        compiler_params=_cparams(("parallel",)),
        name="in_proj",
    )(x, x, x, nw, mod3, mod3, w_in, lnw, lnb, wsp, bsp, qnw, kvnw, wuq, wukv, cc, ss, dcw)


TQ = 2048
TQ_SUB = 256


def _attn_kernel(q_ref, k_ref, v_ref, o_ref):
    k = k_ref[0, 0]
    v = v_ref[0, 0]
    nsub = TQ // TQ_SUB

    def scores(i):
        return _dot_nt(q_ref[0, 0, i * TQ_SUB:(i + 1) * TQ_SUB, :], k)

    s_next = scores(0)
    for i in range(nsub):
        s = s_next
        if i + 1 < nsub:
            s_next = scores(i + 1)
        m = jnp.max(s, axis=-1, keepdims=True)
        p = jnp.exp(s - m)
        l = jnp.sum(p, axis=-1, keepdims=True)
        o = _dot(p.astype(BF), v)
        o_ref[i * TQ_SUB:(i + 1) * TQ_SUB, :] = (o / l).astype(BF)


def _attention(q, k, v):
    nq = S // TQ
    return pl.pallas_call(
        _attn_kernel,
        grid=(B, ML_H, nq),
        in_specs=[pl.BlockSpec((1, 1, TQ, ML_DQ), lambda b, h, i: (b, h, i, 0)),
                  pl.BlockSpec((1, 1, S, ML_DQ), lambda b, h, i: (b, h, 0, 0)),
                  pl.BlockSpec((1, 1, S, ML_V), lambda b, h, i: (b, h, 0, 0))],
        out_specs=pl.BlockSpec((TQ, ML_V), lambda b, h, i: (b * nq + i, h)),
        out_shape=jax.ShapeDtypeStruct((T, ML_H * ML_V), BF),
        compiler_params=_cparams(("parallel", "parallel", "parallel")),
        name="mla_attn",
    )(q, k, v)


def _split3(x):
    hi = x.astype(BF)
    r = x - hi.astype(F32)
    mid = r.astype(BF)
    lo = (r - mid.astype(F32)).astype(BF)
    return hi, mid, lo


def _dn_kernel(q_ref, k_ref, v_ref, z_ref, ba_ref, gpar_ref, onw_ref, o_ref,
               gb_s, dlt_s, lev_s, u_s, wq_s, atk_s, gl_s, os_s, st_s):
    c = DN_C
    nc = DN_NC
    wp = DN_HP * DN_D
    ba = ba_ref[...]
    lane = lax.broadcasted_iota(jnp.int32, (1, 128), 1)
    xg = ba + gpar_ref[0, 0, 1:2, :]
    sp = jnp.maximum(xg, 0.0) + jnp.log1p(jnp.exp(-jnp.abs(xg)))
    gb_s[...] = jnp.where(lane < 4, _sigmoid(ba), -jnp.exp(gpar_ref[0, 0, 0:1, :]) * sp)

    ri = lax.broadcasted_iota(jnp.int32, (c, c), 0)
    ci = lax.broadcasted_iota(jnp.int32, (c, c), 1)
    dlt_s[...] = ri - ci
    xr = jnp.bitwise_xor(ri, ci)
    lev = jnp.zeros((c, c), jnp.int32)
    bs = 1
    while bs < c:
        lev = lev + (xr >= bs).astype(jnp.int32)
        bs *= 2
    lev_s[...] = lev
    n_lev = int(math.log2(c))

    def bdiag(a, b):
        za = jnp.zeros_like(a)
        return jnp.concatenate([jnp.concatenate([a, za], axis=1), jnp.concatenate([za, b], axis=1)], axis=0)

    def prep(i, carry):
        lv = lev_s[...]
        dlt = dlt_s[...]
        lv2 = jnp.concatenate([lv, lv], axis=1)
        incl = (dlt >= 0, dlt <= 0)
        strict = (dlt > 0, dlt < 0)
        tri = tuple(jnp.where(m, 1.0, 0.0).astype(BF) for m in incl)
        eye = jnp.where(lv == 0, 1.0, 0.0).astype(BF)

        cells = []
        for j in range(DN_UNROLL):
            n = i * DN_UNROLL + j
            r0 = pl.multiple_of(n * c, c)
            qpb = q_ref[pl.ds(r0, c), :]
            kpb = k_ref[pl.ds(r0, c), :]
            kp = kpb.astype(F32)
            gbc = gb_s[pl.ds(r0, c), :]
            pair = []
            for hh in range(DN_HP):
                l = slice(hh * DN_D, (hh + 1) * DN_D)
                kc = kp[:, l]
                beta = [jnp.broadcast_to(gbc[:, 2 * d + hh:2 * d + hh + 1], (c, c)) for d in range(2)]
                g = [jnp.broadcast_to(gbc[:, 4 + 2 * d + hh:5 + 2 * d + hh], (c, c)) for d in range(2)]
                kb = [kc * beta[d] for d in range(2)]
                pair.append(dict(n=n, r0=r0, qc=qpb[:, l].astype(F32), kc=kc,
                                 vc=v_ref[pl.ds(r0, c), l].astype(F32), beta=beta, g=g, kb=kb))
            lhs = jnp.concatenate(
                [qpb] + [jnp.concatenate([ce["kb"][d].astype(BF) for ce in pair], axis=1) for d in range(2)], axis=0)
            prod = _dot_nt(lhs, bdiag(kpb[:, :DN_D], kpb[:, DN_D:]))
            for hh, ce in enumerate(pair):
                ce["prod"] = prod[:, hh * c:(hh + 1) * c]
                cells.append(ce)

        cum = []
        for d in range(2):
            rg = jnp.concatenate([jnp.where(strict[d], ce["g"][d], 0.0) for ce in cells], axis=1)
            cum.append(sum(_dot(tri[d], part) for part in _split3(rg)))

        items = []
        for j in range(DN_UNROLL):
            for d in range(2):
                low, rhs, at, qdec, kdec, gl = [], [], [], [], [], []
                for hh in range(DN_HP):
                    ce = cells[j * DN_HP + hh]
                    diff = cum[d][:, (j * DN_HP + hh) * c:(j * DN_HP + hh + 1) * c]
                    e0 = 0 if d == 0 else c - 1
                    gc_b = jnp.broadcast_to(diff[:, e0:e0 + 1], (c, c)) + ce["g"][d][e0:e0 + 1, :]
                    decay = jnp.where(incl[d], jnp.exp(jnp.where(incl[d], diff, 0.0)), 0.0)
                    glast = gc_b[c - 1:c, :] if d == 0 else gc_b[0:1, :]
                    e_gc = jnp.exp(gc_b)
                    low.append(jnp.where(strict[d], ce["prod"][(1 + d) * c:(2 + d) * c] * decay, 0.0))
                    at.append(jnp.where(incl[d], ce["prod"][0:c] * decay, 0.0).astype(BF))
                    qdec.append((ce["qc"] * e_gc).astype(BF))
                    kdec.append((ce["kc"] * jnp.exp(glast - gc_b)).astype(BF))
                    rhs.append(jnp.concatenate([ce["vc"] * ce["beta"][d], ce["kb"][d] * e_gc], axis=1).astype(BF))
                    gl.append(jnp.exp(glast))
                lowp = jnp.concatenate(low, axis=1)
                n = cells[j * DN_HP]["n"]
                r0 = cells[j * DN_HP]["r0"]
                wq_s[d, n, c:2 * c, :] = jnp.concatenate(qdec, axis=1)
                atk_s[d, n, 0:c, :] = jnp.concatenate(at, axis=1)
                atk_s[d, n, c:2 * c, :] = _dot_nt(eye, jnp.concatenate(kdec, axis=0)).astype(BF)
                gl_s[d, n] = jnp.broadcast_to(jnp.concatenate(gl, axis=1), (8, wp))
                x = jnp.where(lv2 == 0, 1.0, 0.0) - jnp.where(lv2 == 1, lowp, 0.0)
                items.append(dict(d=d, n=n, r0=r0, lowp=lowp, rhs=rhs, x=x))

        for k in range(2, n_lev + 1):
            xb = [it["x"].astype(BF) for it in items]
            t1 = []
            for it, b in zip(items, xb):
                e = jnp.where(lv2 == k, it["lowp"], 0.0).astype(BF)
                t1.append(_dot(b, bdiag(e[:, :c], e[:, c:])).astype(BF))
            t2 = [_dot(t, bdiag(b[:, :c], b[:, c:])) for t, b in zip(t1, xb)]
            for it, t in zip(items, t2):
                it["x"] = it["x"] - t

        sols = [_dot(it["x"].astype(BF), bdiag(it["rhs"][0], it["rhs"][1])) for it in items]
        for it, sol in zip(items, sols):
            d, n, r0 = it["d"], it["n"], it["r0"]
            u_s[d, pl.ds(r0, c), :] = jnp.concatenate([sol[:, 0:DN_D], sol[:, 2 * DN_D:3 * DN_D]], axis=1)
            wq_s[d, n, 0:c, :] = jnp.concatenate([sol[:, DN_D:2 * DN_D], sol[:, 3 * DN_D:]], axis=1).astype(BF)
        return carry

    lax.fori_loop(0, nc // DN_UNROLL, prep, 0)

    st_s[...] = jnp.zeros((2, DN_D, wp), F32)

    def scan(i, carry):
        ns = (i, nc - 1 - i)
        sts = [st_s[d] for d in range(2)]
        prs = []
        for d in range(2):
            sb = sts[d].astype(BF)
            prs.append(_dot(wq_s[d, ns[d]], bdiag(sb[:, :DN_D], sb[:, DN_D:])))
        r2s = []
        for d in range(2):
            r0 = pl.multiple_of(ns[d] * c, c)
            vn = (u_s[d, pl.ds(r0, c), :] - prs[d][:c]).astype(BF)
            r2s.append(_dot(atk_s[d, ns[d]], bdiag(vn[:, :DN_D], vn[:, DN_D:])))
        for d in range(2):
            r0 = pl.multiple_of(ns[d] * c, c)
            os_s[d, pl.ds(r0, c), :] = prs[d][c:] + r2s[d][:c]
            st_s[d] = sts[d] * gl_s[d, ns[d]][0:1, :] + r2s[d][c:]
        return carry

    lax.fori_loop(0, nc, scan, 0)

    o = os_s[0] + os_s[1]
    sil = _silu(z_ref[...].astype(F32))
    for hh in range(DN_HP):
        l = slice(hh * DN_D, (hh + 1) * DN_D)
        o_ref[:, l] = (_rms(o[:, l], onw_ref[0]) * sil[:, l]).astype(BF)


def _deltanet(l, qkv, z, ba, gpar, onw):
    c, nc, wp = DN_C, DN_NC, DN_HP * DN_D
    npair = DN_H // DN_HP
    col = lambda off: pl.BlockSpec((S, wp), lambda b, p: (b, off + p))
    return pl.pallas_call(
        _dn_kernel,
        grid=(B, npair),
        in_specs=[col(0), col(npair), col(2 * npair), col(0),
                  pl.BlockSpec((S, 128), lambda b, p: (b, p)),
                  pl.BlockSpec((1, 1, 2, 128), lambda b, p: (l, p, 0, 0)), _layer_spec(l, (1, DN_D))],
        out_specs=col(0),
        out_shape=jax.ShapeDtypeStruct((T, DN_W), BF),
        scratch_shapes=[pltpu.VMEM((S, 128), F32),
                        pltpu.VMEM((c, c), jnp.int32), pltpu.VMEM((c, c), jnp.int32),
                        pltpu.VMEM((2, S, wp), F32),
                        pltpu.VMEM((2, nc, 2 * c, wp), BF), pltpu.VMEM((2, nc, 2 * c, wp), BF),
                        pltpu.VMEM((2, nc, 8, wp), F32),
                        pltpu.VMEM((2, S, wp), F32), pltpu.VMEM((2, DN_D, wp), F32)],
        compiler_params=_cparams(("parallel", "parallel")),
        name="deltanet",
    )(qkv, qkv, qkv, z, ba, gpar, onw)


TM_MG = 512


def _merge_kernel(oa_ref, ob_ref, oc_ref, gate_ref, bg_ref, wbr_ref, wo_ref, x_ref, nw1_ref, gt1_ref,
                  nw2_ref, sc2_ref, sh2_ref, x1_ref, h2_ref):
    branches = (oa_ref, ob_ref, oc_ref)
    halves = [slice(k * TM_MG // 2, (k + 1) * TM_MG // 2) for k in range(2)]

    def ups(r):
        return [_dot(o_ref[r, :], wbr_ref[0, n]) for n, o_ref in enumerate(branches)]

    def gated(r, up):
        acc = None
        for n in range(3):
            g = _sigmoid(gate_ref[r, n * D:(n + 1) * D].astype(F32) + bg_ref[0, n:n + 1, :])
            acc = g * up[n] if acc is None else acc + g * up[n]
        return acc.astype(BF)

    def finish(r, y):
        x1 = x_ref[r, :] + gt1_ref[0] * _rms(y, nw1_ref[0])
        x1_ref[r, :] = x1
        h2_ref[r, :] = (_rms(x1, nw2_ref[0]) * (1.0 + sc2_ref[0]) + sh2_ref[0]).astype(BF)

    up0 = ups(halves[0])
    up1 = ups(halves[1])
    y0 = _dot(gated(halves[0], up0), wo_ref[0])
    y1 = _dot(gated(halves[1], up1), wo_ref[0])
    finish(halves[0], y0)
    finish(halves[1], y1)


def _merge(l, oa, ob, oc, gate, bg, wbr, wo, x, mod3, nw):
    tm = TM_MG
    per_b = S // tm
    row = lambda n: pl.BlockSpec((tm, n), lambda i: (i, 0))
    modspec = lambda k: pl.BlockSpec((1, 1, D), lambda i: (l * B * 6 + (i // per_b) * 6 + k, 0, 0))
    return pl.pallas_call(
        _merge_kernel,
        grid=(T // tm,),
        in_specs=[row(512), row(512), row(512), row(3 * D), _layer_spec(l, (3, D)),
                  _layer_spec(l, (3, 512, D)), _layer_spec(l, (D, D)), row(D), _norm_spec(l, 1),
                  modspec(2), _norm_spec(l, 2), modspec(4), modspec(3)],
        out_specs=[row(D), row(D)],
        out_shape=[jax.ShapeDtypeStruct((T, D), F32), jax.ShapeDtypeStruct((T, D), BF)],
        compiler_params=_cparams(("parallel",)),
        name="merge",
    )(oa, ob, oc, gate, bg, wbr, wo, x, nw, mod3, nw, mod3, mod3)


TM_FF = 512
TF = 256
FF_HALO = 16
FF_DOWN_TILES = 6


def _ffn_kernel(h_ref, x1_ref, wup_ref, cw_ref, cb_ref, wdn_ref, nw_ref, gt_ref, o_ref, act_s):
    i = pl.program_id(1)
    nblk = S // TM_FF
    r0 = pl.multiple_of(i * TM_FF, TM_FF)
    top0 = pl.multiple_of(jnp.maximum(r0 - FF_HALO, 0), FF_HALO)
    bot0 = pl.multiple_of(jnp.minimum(r0 + TM_FF, S - FF_HALO), FF_HALO)
    zero = jnp.zeros((FF_HALO, D), BF)
    top = jnp.where(i > 0, h_ref[pl.ds(top0, FF_HALO), :], zero)
    bot = jnp.where(i < nblk - 1, h_ref[pl.ds(bot0, FF_HALO), :], zero)
    slab = jnp.concatenate([top, h_ref[pl.ds(r0, TM_FF), :], bot], axis=0)

    def up(col):
        return _dot(slab, wup_ref[0, :, col:col + TF])

    def conv(p, col):
        cw = cw_ref[0, :, col:col + TF]
        h0 = FF_HALO
        return (p[h0 - 1:h0 - 1 + TM_FF] * cw[0:1] + p[h0:h0 + TM_FF] * cw[1:2]
                + p[h0 + 1:h0 + 1 + TM_FF] * cw[2:3] + cb_ref[0, :, col:col + TF])

    nf = D_FF // TF
    nxt = (up(0), up(D_FF))
    y = None
    for j in range(nf):
        pa, pb = nxt
        if j + 1 < nf:
            nxt = (up((j + 1) * TF), up(D_FF + (j + 1) * TF))
        a = conv(pa, j * TF)
        b = conv(pb, D_FF + j * TF)
        act_s[:, j * TF:(j + 1) * TF] = (_silu(a) * b).astype(BF)
        if (j + 1) % FF_DOWN_TILES == 0 or j + 1 == nf:
            k0 = (j // FF_DOWN_TILES) * FF_DOWN_TILES * TF
            part = _dot(act_s[:, k0:(j + 1) * TF], wdn_ref[0, k0:(j + 1) * TF, :])
            y = part if y is None else y + part
    o_ref[...] = x1_ref[...] + gt_ref[0] * _rms(y, nw_ref[0])


def _ffn(l, h2, x1, w_up, conv_w, conv_b, w_down, mod3, nw):
    nblk = S // TM_FF
    row = pl.BlockSpec((TM_FF, D), lambda b, i: (b * nblk + i, 0))
    return pl.pallas_call(
        _ffn_kernel,
        grid=(B, nblk),
        in_specs=[pl.BlockSpec((S, D), lambda b, i: (b, 0)), row,
                  _layer_spec(l, (D, 2 * D_FF)), _layer_spec(l, (3, 2 * D_FF)), _layer_spec(l, (1, 2 * D_FF)),
                  _layer_spec(l, (D_FF, D)), _norm_spec(l, 3),
                  pl.BlockSpec((1, 1, D), lambda b, i: (l * B * 6 + b * 6 + 5, 0, 0))],
        out_specs=row,
        out_shape=jax.ShapeDtypeStruct((T, D), F32),
        scratch_shapes=[pltpu.VMEM((TM_FF, D_FF), BF)],
        compiler_params=_cparams(("parallel", "arbitrary")),
        name="conv_ffn",
    )(h2, x1, w_up, conv_w, conv_b, w_down, nw, mod3)


def _pair_lanes(t):
    t = t.reshape(t.shape[:-1] + (2, DN_H // DN_HP, DN_HP))
    t = jnp.moveaxis(t, -2, -3)
    return t.reshape(t.shape[:-2] + (2 * DN_HP,))


def _pair_rows(t):
    t = t.reshape(DEPTH, 2, DN_H // DN_HP, DN_HP, D)
    return jnp.moveaxis(t, 2, 1).reshape(DEPTH, DN_H // DN_HP, 2 * DN_HP, D)


def _prep_w_in(w_in):
    wt = jnp.swapaxes(w_in, 1, 2)
    a_uv, qkv, z, beta, alpha, cq, ckv, kr, gate = jnp.split(
        wt, [1024, 2560, 3072, 3080, 3088, 3472, 3728, 3792], axis=1)
    kr_sw = jnp.concatenate([kr[:, 32:], kr[:, :32]], axis=1)
    pad = jnp.zeros((DEPTH, DN_H // DN_HP, 120, D), w_in.dtype)
    ba = jnp.concatenate([_pair_rows(beta), _pair_rows(alpha), pad], axis=2).reshape(DEPTH, 256, D)
    return jnp.concatenate([a_uv, qkv, z, cq, ckv, kr, kr_sw, ba, gate], axis=1).astype(BF)


def _prep_gpar(a_log, dt_bias):
    def lanes(t):
        tp = _pair_lanes(t.reshape(DEPTH, 2 * DN_H))
        z4 = jnp.zeros((DEPTH, DN_H // DN_HP, 4), F32)
        z120 = jnp.zeros((DEPTH, DN_H // DN_HP, 120), F32)
        return jnp.concatenate([z4, tp, z120], axis=-1)
    return jnp.stack([lanes(a_log), lanes(dt_bias)], axis=2)


def _prep_w_uq(w_uq):
    wq = w_uq.reshape(DEPTH, ML_QR, ML_H, ML_DQ)
    nope = wq[..., :ML_NOPE].reshape(DEPTH, ML_QR, ML_H * ML_NOPE)
    pe = wq[..., ML_NOPE:]
    pe_sw = jnp.concatenate([pe[..., 32:], pe[..., :32]], axis=-1)
    return jnp.concatenate([nope, pe.reshape(DEPTH, ML_QR, 256), pe_sw.reshape(DEPTH, ML_QR, 256)],
                           axis=-1).astype(BF)


def _prep_w_ukv(w_ukv):
    wk = w_ukv.reshape(DEPTH, ML_KVR, ML_H, ML_NOPE + ML_V)
    return jnp.concatenate([wk[..., :ML_NOPE].reshape(DEPTH, ML_KVR, 512),
                            wk[..., ML_NOPE:].reshape(DEPTH, ML_KVR, 512)], axis=-1).astype(BF)


@jax.jit
def _forward(x, c, positions, w_ada, b_ada, norm_w, w_in, b_gate, a_ln_w, a_ln_b, a_w_sp, a_b_sp, dn_conv_w,
             dn_a_log, dn_dt_bias, dn_o_norm_w, mla_q_norm_w, mla_kv_norm_w, mla_w_uq, mla_w_ukv, w_branch,
             w_o, ffn_w_up, ffn_conv_w, ffn_conv_b, ffn_w_down):
    mod = _modulation(c, w_ada, b_ada)
    cc, ss = _rope_tables(positions)
    w_in_r = _prep_w_in(w_in)
    wuq_r = _prep_w_uq(mla_w_uq)
    wukv_r = _prep_w_ukv(mla_w_ukv)
    wsp = a_w_sp.astype(BF)
    bsp = jnp.broadcast_to(a_b_sp[..., None], (DEPTH, 4, 128, 128))
    wbr = w_branch.astype(BF)
    wo = w_o.astype(BF)
    wup = ffn_w_up.astype(BF)
    wdn = ffn_w_down.astype(BF)
    gpar = _prep_gpar(dn_a_log, dn_dt_bias)
    mod3 = mod.reshape(DEPTH * B * 6, 1, D)
    lnw = a_ln_w.reshape(DEPTH, 1, A_W)
    lnb = a_ln_b.reshape(DEPTH, 1, A_W)
    qnw = mla_q_norm_w.reshape(DEPTH, 1, ML_QR)
    kvnw = mla_kv_norm_w.reshape(DEPTH, 1, ML_KVR)
    onw = dn_o_norm_w.reshape(DEPTH, 1, DN_D)
    cb = ffn_conv_b.reshape(DEPTH, 1, 2 * D_FF)
    nw = norm_w.reshape(DEPTH * 4, 1, D)

    xf = x.reshape(T, D)
    for l in range(DEPTH):
        oa, qkv, z, ba, gate, q, k, v = _in_proj(l, xf, mod3, nw, w_in_r, lnw, lnb, wsp, bsp, qnw, kvnw,
                                                 wuq_r, wukv_r, cc, ss, dn_conv_w)
        ob = _deltanet(l, qkv, z, ba, gpar, onw)
        oc = _attention(q, k, v)
        x1, h2 = _merge(l, oa, ob, oc, gate, b_gate, wbr, wo, xf, mod3, nw)
        xf = _ffn(l, h2, x1, wup, ffn_conv_w, cb, wdn, mod3, nw)
    return xf.reshape(B, S, D)


def kernel(x, c, positions, w_ada, b_ada, norm_w, w_in, b_gate, a_ln_w, a_ln_b, a_w_sp, a_b_sp, dn_conv_w,
           dn_a_log, dn_dt_bias, dn_o_norm_w, mla_q_norm_w, mla_kv_norm_w, mla_w_uq, mla_w_ukv, w_branch, w_o,
           ffn_w_up, ffn_conv_w, ffn_conv_b, ffn_w_down):
    return _forward(x, c, positions, w_ada, b_ada, norm_w, w_in, b_gate, a_ln_w, a_ln_b, a_w_sp, a_b_sp,
                    dn_conv_w, dn_a_log, dn_dt_bias, dn_o_norm_w, mla_q_norm_w, mla_kv_norm_w, mla_w_uq,
                    mla_w_ukv, w_branch, w_o, ffn_w_up, ffn_conv_w, ffn_conv_b, ffn_w_down)
```

```python
import math

import jax
import jax.numpy as jnp
from jax import lax
from jax.experimental import pallas as pl
from jax.experimental.pallas import tpu as pltpu

F32 = jnp.float32
BF = jnp.bfloat16

D = 1024
B = 8
S = 2048
DEPTH = 4
T = B * S

A_W = 512
A_CHUNK = 128
DN_H = 4
DN_D = 128
DN_W = 512
DN_CONV = 5
DN_C = 128
DN_NC = S // DN_C
DN_HP = 2
DN_UNROLL = 4
ML_H = 4
ML_QR = 384
ML_KVR = 256
ML_NOPE = 128
ML_ROPE = 64
ML_V = 128
ML_DQ = ML_NOPE + ML_ROPE
ROPE_THETA = 10000.0
D_FF = 2816
RMS_EPS = 1e-6
LN_EPS = 1e-5

C_A = 0
C_QKV = 1024
C_Z = 2560
C_CQ = 3072
C_CKV = 3456
C_KR = 3712
C_BA = 3840
C_GATE = 4096
NP_IN = 7168

VMEM_LIMIT = 56 * 1024 * 1024


def _cparams(sem):
    return pltpu.CompilerParams(dimension_semantics=sem, vmem_limit_bytes=VMEM_LIMIT)


def _dot(a, b):
    return jnp.dot(a, b, preferred_element_type=F32)


def _dot_nt(a, b):
    return lax.dot_general(a, b, (((1,), (1,)), ((), ())), preferred_element_type=F32)


def _rms(x, w, eps=RMS_EPS):
    return x * lax.rsqrt(jnp.mean(x * x, axis=-1, keepdims=True) + eps) * w


def _sigmoid(x):
    return 1.0 / (1.0 + jnp.exp(-x))


def _silu(x):
    return x * _sigmoid(x)


def _gelu_tanh(x):
    c = math.sqrt(2.0 / math.pi)
    return 0.5 * x * (1.0 + jnp.tanh(c * (x + 0.044715 * (x * x * x))))


def _layer_spec(l, tail, idx=None):
    idx = (0,) * len(tail) if idx is None else idx
    return pl.BlockSpec((1,) + tuple(tail), lambda *_: (l,) + tuple(idx), pipeline_mode=pl.Buffered(1))


def _norm_spec(l, k):
    return pl.BlockSpec((1, 1, D), lambda *_: (l * 4 + k, 0, 0), pipeline_mode=pl.Buffered(1))


def _mod_kernel(c_ref, w_ref, b_ref, o_ref):
    c = c_ref[...]
    ca = _silu(c).astype(BF)
    o_ref[0] = _dot(ca, w_ref[0].astype(BF)) + b_ref[0]


def _modulation(c, w_ada, b_ada):
    tn = 1536
    return pl.pallas_call(
        _mod_kernel,
        grid=(DEPTH, 6 * D // tn),
        in_specs=[pl.BlockSpec((B, D), lambda l, j: (0, 0)),
                  pl.BlockSpec((1, D, tn), lambda l, j: (l, 0, j)),
                  pl.BlockSpec((1, 1, tn), lambda l, j: (l, 0, j))],
        out_specs=pl.BlockSpec((1, B, tn), lambda l, j: (l, 0, j)),
        out_shape=jax.ShapeDtypeStruct((DEPTH, B, 6 * D), F32),
        compiler_params=_cparams(("parallel", "parallel")),
        name="adaln_mod",
    )(c, w_ada, b_ada.reshape(DEPTH, 1, 6 * D))


def _rope_kernel(pos_ref, invf_ref, sign_ref, cc_ref, ss_ref):
    ang = pos_ref[...].astype(F32) * invf_ref[...]
    c = jnp.cos(ang)
    s = jnp.sin(ang) * sign_ref[...]
    cc_ref[...] = jnp.concatenate([c, c], axis=-1)
    ss_ref[...] = jnp.concatenate([s, s], axis=-1)


def _rope_tables(positions):
    half = ML_ROPE // 2
    inv_freq = ROPE_THETA ** (-jnp.arange(0, ML_ROPE, 2, dtype=F32) / ML_ROPE)
    invf = jnp.tile(inv_freq, 4).reshape(1, 128)
    sign = jnp.tile(jnp.concatenate([-jnp.ones((half,), F32), jnp.ones((half,), F32)]), 2).reshape(1, 128)
    return pl.pallas_call(
        _rope_kernel,
        grid=(B,),
        in_specs=[pl.BlockSpec((S, 1), lambda b: (b, 0)),
                  pl.BlockSpec((1, 128), lambda b: (0, 0)),
                  pl.BlockSpec((1, 128), lambda b: (0, 0))],
        out_specs=[pl.BlockSpec((S, 256), lambda b: (b, 0)),
                   pl.BlockSpec((S, 256), lambda b: (b, 0))],
        out_shape=[jax.ShapeDtypeStruct((T, 256), F32), jax.ShapeDtypeStruct((T, 256), F32)],
        compiler_params=_cparams(("parallel",)),
        name="rope_tables",
    )(positions.reshape(T, 1), invf, sign)


TM_IN = 512
IN_HALO = 16


def _in_kernel(x_ref, xt_ref, xb_ref, nw_ref, sc_ref, sh_ref, w_ref, lnw_ref, lnb_ref, wsp_ref, bsp_ref,
               qnw_ref, kvnw_ref, wuq_ref, wukv_ref, cc_ref, ss_ref, dcw_ref,
               oa_ref, qkv_ref, z_ref, ba_ref, gate_ref, q_ref, k_ref, v_ref):
    i = pl.program_id(0)
    per_b = S // TM_IN

    def modulate(xx):
        return (_rms(xx, nw_ref[0]) * (1.0 + sc_ref[0]) + sh_ref[0]).astype(BF)

    hb = modulate(x_ref[...])

    def seg(c0, n):
        return _dot_nt(hb, w_ref[0, c0:c0 + n, :])

    zero = jnp.zeros((IN_HALO, D), BF)
    top = jnp.where(i % per_b > 0, modulate(xt_ref[...]), zero)
    bot = jnp.where(i % per_b < per_b - 1, modulate(xb_ref[...]), zero)
    slab = jnp.concatenate([top, hb, bot], axis=0)
    def qkv_mm(part):
        c0 = part * DN_W
        return _dot_nt(slab, w_ref[0, C_QKV + c0:C_QKV + c0 + DN_W, :])

    def qkv_finish(part, p):
        c0 = part * DN_W
        acc = None
        rows = TM_IN + 2 * IN_HALO
        for t in range(DN_CONV):
            sh = DN_CONV // 2 - t
            pt = p if sh == 0 else pltpu.roll(p, shift=sh % rows, axis=0)
            term = pt[IN_HALO:IN_HALO + TM_IN] * dcw_ref[0, t:t + 1, c0:c0 + DN_W]
            acc = term if acc is None else acc + term
        y = _silu(acc)
        for hh in range(DN_H):
            l = slice(hh * DN_D, (hh + 1) * DN_D)
            yh = y[:, l]
            if part == 0:
                yh = yh * (lax.rsqrt(jnp.sum(yh * yh, axis=-1, keepdims=True) + 1e-6) * (DN_D ** -0.5))
            elif part == 1:
                yh = yh * lax.rsqrt(jnp.sum(yh * yh, axis=-1, keepdims=True) + 1e-6)
            qkv_ref[:, c0 + hh * DN_D:c0 + (hh + 1) * DN_D] = yh.astype(BF)

    def gate_mm(n):
        return seg(C_GATE + n * D, D)

    def gate_store(n, gt):
        gate_ref[:, n * D:(n + 1) * D] = gt.astype(BF)

    p_q = qkv_mm(0)
    p_k = qkv_mm(1)
    qkv_finish(0, p_q)
    p_v = qkv_mm(2)
    qkv_finish(1, p_k)
    a_raw = seg(C_A, 2 * A_W)
    qkv_finish(2, p_v)
    g0 = gate_mm(0)

    a = _gelu_tanh(a_raw)
    u = a[:, :A_W]
    v = a[:, A_W:]
    mu = jnp.mean(v, axis=-1, keepdims=True)
    vc = v - mu
    var = jnp.mean(vc * vc, axis=-1, keepdims=True)
    vn = (vc * lax.rsqrt(var + LN_EPS) * lnw_ref[0] + lnb_ref[0]).astype(BF)
    lowrank = seg(C_CQ, ML_QR + ML_KVR + 128)
    gate_store(0, g0)
    for c in range(TM_IN // A_CHUNK):
        r = slice(c * A_CHUNK, (c + 1) * A_CHUNK)
        for g in range(4):
            l = slice(g * 128, (g + 1) * 128)
            mixed = _dot(wsp_ref[0, g], vn[r, l]) + bsp_ref[0, g]
            oa_ref[r, l] = (u[r, l] * mixed).astype(BF)

    g1 = gate_mm(1)

    cc = cc_ref[...]
    ss = ss_ref[...]
    cqn = _rms(lowrank[:, :ML_QR], qnw_ref[0]).astype(BF)
    ckvn = _rms(lowrank[:, ML_QR:ML_QR + ML_KVR], kvnw_ref[0]).astype(BF)
    krr = lowrank[:, ML_QR + ML_KVR:]
    qall = _dot(cqn, wuq_ref[0]) * (ML_DQ ** -0.5)
    kvall = _dot(ckvn, wukv_ref[0])
    gate_store(1, g1)
    g2 = gate_mm(2)
    q_rot = qall[:, 512:768] * cc + qall[:, 768:1024] * ss
    k_pe = (krr[:, :64] * cc[:, :64] + krr[:, 64:] * ss[:, :64]).astype(BF)
    for hh in range(ML_H):
        l = slice(hh * 128, (hh + 1) * 128)
        q_ref[0, hh, :, 0:ML_NOPE] = qall[:, l].astype(BF)
        q_ref[0, hh, :, ML_NOPE:ML_DQ] = q_rot[:, hh * 64:(hh + 1) * 64].astype(BF)
        k_ref[0, hh, :, 0:ML_NOPE] = kvall[:, l].astype(BF)
        k_ref[0, hh, :, ML_NOPE:ML_DQ] = k_pe
        v_ref[0, hh] = kvall[:, 512 + hh * 128:512 + (hh + 1) * 128].astype(BF)
    zb = seg(C_Z, DN_W)
    bab = seg(C_BA, 256)
    gate_store(2, g2)
    z_ref[...] = zb.astype(BF)
    ba_ref[...] = bab


def _in_proj(l, x, mod3, nw, w_in, lnw, lnb, wsp, bsp, qnw, kvnw, wuq, wukv, cc, ss, dcw):
    tm = TM_IN
    per_b = S // tm
    nh = tm // IN_HALO
    row = lambda n: pl.BlockSpec((tm, n), lambda i: (i, 0))
    modspec = lambda k: pl.BlockSpec((1, 1, D), lambda i: (l * B * 6 + (i // per_b) * 6 + k, 0, 0))
    hspec = lambda n: pl.BlockSpec((1, ML_H, tm, n), lambda i: (i // per_b, 0, i % per_b, 0))
    return pl.pallas_call(
        _in_kernel,
        grid=(T // tm,),
        in_specs=[row(D),
                  pl.BlockSpec((IN_HALO, D), lambda i: (jnp.maximum(i * nh - 1, 0), 0)),
                  pl.BlockSpec((IN_HALO, D), lambda i: (jnp.minimum((i + 1) * nh, T // IN_HALO - 1), 0)),
                  _norm_spec(l, 0), modspec(1), modspec(0), _layer_spec(l, (NP_IN, D)),
                  _layer_spec(l, (1, A_W)), _layer_spec(l, (1, A_W)), _layer_spec(l, (4, 128, 128)),
                  _layer_spec(l, (4, 128, 128)), _layer_spec(l, (1, ML_QR)), _layer_spec(l, (1, ML_KVR)),
                  _layer_spec(l, (ML_QR, 1024)), _layer_spec(l, (ML_KVR, 1024)), row(256), row(256),
                  _layer_spec(l, (DN_CONV, 3 * DN_W))],
        out_specs=[row(A_W), row(3 * DN_W), row(DN_W), row(256), row(3 * D),
                   hspec(ML_DQ), hspec(ML_DQ), hspec(ML_V)],
        out_shape=[jax.ShapeDtypeStruct((T, A_W), BF), jax.ShapeDtypeStruct((T, 3 * DN_W), BF),
                   jax.ShapeDtypeStruct((T, DN_W), BF), jax.ShapeDtypeStruct((T, 256), F32),
                   jax.ShapeDtypeStruct((T, 3 * D), BF),
                   jax.ShapeDtypeStruct((B, ML_H, S, ML_DQ), BF),
                   jax.ShapeDtypeStruct((B, ML_H, S, ML_DQ), BF),
                   jax.ShapeDtypeStruct((B, ML_H, S, ML_V), BF)],
        compiler_params=_cparams(("parallel",)),
        name="in_proj",
    )(x, x, x, nw, mod3, mod3, w_in, lnw, lnb, wsp, bsp, qnw, kvnw, wuq, wukv, cc, ss, dcw)


TQ = 2048
TQ_SUB = 256


def _attn_kernel(q_ref, k_ref, v_ref, o_ref):
    k = k_ref[0, 0]
    v = v_ref[0, 0]
    nsub = TQ // TQ_SUB

    def scores(i):
        return _dot_nt(q_ref[0, 0, i * TQ_SUB:(i + 1) * TQ_SUB, :], k)

    s_next = scores(0)
    for i in range(nsub):
        s = s_next
        if i + 1 < nsub:
            s_next = scores(i + 1)
        m = jnp.max(s, axis=-1, keepdims=True)
        p = jnp.exp(s - m)
        l = jnp.sum(p, axis=-1, keepdims=True)
        o = _dot(p.astype(BF), v)
        o_ref[i * TQ_SUB:(i + 1) * TQ_SUB, :] = (o / l).astype(BF)


def _attention(q, k, v):
    nq = S // TQ
    return pl.pallas_call(
        _attn_kernel,
        grid=(B, ML_H, nq),
        in_specs=[pl.BlockSpec((1, 1, TQ, ML_DQ), lambda b, h, i: (b, h, i, 0)),
                  pl.BlockSpec((1, 1, S, ML_DQ), lambda b, h, i: (b, h, 0, 0)),
                  pl.BlockSpec((1, 1, S, ML_V), lambda b, h, i: (b, h, 0, 0))],
        out_specs=pl.BlockSpec((TQ, ML_V), lambda b, h, i: (b * nq + i, h)),
        out_shape=jax.ShapeDtypeStruct((T, ML_H * ML_V), BF),
        compiler_params=_cparams(("parallel", "parallel", "parallel")),
        name="mla_attn",
    )(q, k, v)


def _split3(x):
    hi = x.astype(BF)
    r = x - hi.astype(F32)
    mid = r.astype(BF)
    lo = (r - mid.astype(F32)).astype(BF)
    return hi, mid, lo


def _dn_kernel(q_ref, k_ref, v_ref, z_ref, ba_ref, gpar_ref, onw_ref, o_ref,
               gb_s, dlt_s, lev_s, u_s, wq_s, atk_s, gl_s, os_s, st_s):
    c = DN_C
    nc = DN_NC
    wp = DN_HP * DN_D
    ba = ba_ref[...]
    lane = lax.broadcasted_iota(jnp.int32, (1, 128), 1)
    xg = ba + gpar_ref[0, 0, 1:2, :]
    sp = jnp.maximum(xg, 0.0) + jnp.log1p(jnp.exp(-jnp.abs(xg)))
    gb_s[...] = jnp.where(lane < 4, _sigmoid(ba), -jnp.exp(gpar_ref[0, 0, 0:1, :]) * sp)

    ri = lax.broadcasted_iota(jnp.int32, (c, c), 0)
    ci = lax.broadcasted_iota(jnp.int32, (c, c), 1)
    dlt_s[...] = ri - ci
    xr = jnp.bitwise_xor(ri, ci)
    lev = jnp.zeros((c, c), jnp.int32)
    bs = 1
    while bs < c:
        lev = lev + (xr >= bs).astype(jnp.int32)
        bs *= 2
    lev_s[...] = lev
    n_lev = int(math.log2(c))

    def bdiag(a, b):
        za = jnp.zeros_like(a)
        return jnp.concatenate([jnp.concatenate([a, za], axis=1), jnp.concatenate([za, b], axis=1)], axis=0)

    def prep(i, carry):
        lv = lev_s[...]
        dlt = dlt_s[...]
        lv2 = jnp.concatenate([lv, lv], axis=1)
        incl = (dlt >= 0, dlt <= 0)
        strict = (dlt > 0, dlt < 0)
        tri = tuple(jnp.where(m, 1.0, 0.0).astype(BF) for m in incl)
        eye = jnp.where(lv == 0, 1.0, 0.0).astype(BF)

        cells = []
        for j in range(DN_UNROLL):
            n = i * DN_UNROLL + j
            r0 = pl.multiple_of(n * c, c)
            qpb = q_ref[pl.ds(r0, c), :]
            kpb = k_ref[pl.ds(r0, c), :]
            kp = kpb.astype(F32)
            gbc = gb_s[pl.ds(r0, c), :]
            pair = []
            for hh in range(DN_HP):
                l = slice(hh * DN_D, (hh + 1) * DN_D)
                kc = kp[:, l]
                beta = [jnp.broadcast_to(gbc[:, 2 * d + hh:2 * d + hh + 1], (c, c)) for d in range(2)]
                g = [jnp.broadcast_to(gbc[:, 4 + 2 * d + hh:5 + 2 * d + hh], (c, c)) for d in range(2)]
                kb = [kc * beta[d] for d in range(2)]
                pair.append(dict(n=n, r0=r0, qc=qpb[:, l].astype(F32), kc=kc,
                                 vc=v_ref[pl.ds(r0, c), l].astype(F32), beta=beta, g=g, kb=kb))
            lhs = jnp.concatenate(
                [qpb] + [jnp.concatenate([ce["kb"][d].astype(BF) for ce in pair], axis=1) for d in range(2)], axis=0)
            prod = _dot_nt(lhs, bdiag(kpb[:, :DN_D], kpb[:, DN_D:]))
            for hh, ce in enumerate(pair):
                ce["prod"] = prod[:, hh * c:(hh + 1) * c]
                cells.append(ce)

        cum = []
        for d in range(2):
            rg = jnp.concatenate([jnp.where(strict[d], ce["g"][d], 0.0) for ce in cells], axis=1)
            cum.append(sum(_dot(tri[d], part) for part in _split3(rg)))

        items = []
        for j in range(DN_UNROLL):
            for d in range(2):
                low, rhs, at, qdec, kdec, gl = [], [], [], [], [], []
                for hh in range(DN_HP):
                    ce = cells[j * DN_HP + hh]
                    diff = cum[d][:, (j * DN_HP + hh) * c:(j * DN_HP + hh + 1) * c]
                    e0 = 0 if d == 0 else c - 1
                    gc_b = jnp.broadcast_to(diff[:, e0:e0 + 1], (c, c)) + ce["g"][d][e0:e0 + 1, :]
                    decay = jnp.where(incl[d], jnp.exp(jnp.where(incl[d], diff, 0.0)), 0.0)
                    glast = gc_b[c - 1:c, :] if d == 0 else gc_b[0:1, :]
                    e_gc = jnp.exp(gc_b)
                    low.append(jnp.where(strict[d], ce["prod"][(1 + d) * c:(2 + d) * c] * decay, 0.0))
                    at.append(jnp.where(incl[d], ce["prod"][0:c] * decay, 0.0).astype(BF))
                    qdec.append((ce["qc"] * e_gc).astype(BF))
                    kdec.append((ce["kc"] * jnp.exp(glast - gc_b)).astype(BF))
                    rhs.append(jnp.concatenate([ce["vc"] * ce["beta"][d], ce["kb"][d] * e_gc], axis=1).astype(BF))
                    gl.append(jnp.exp(glast))
                lowp = jnp.concatenate(low, axis=1)
                n = cells[j * DN_HP]["n"]
                r0 = cells[j * DN_HP]["r0"]
                wq_s[d, n, c:2 * c, :] = jnp.concatenate(qdec, axis=1)
                atk_s[d, n, 0:c, :] = jnp.concatenate(at, axis=1)
                atk_s[d, n, c:2 * c, :] = _dot_nt(eye, jnp.concatenate(kdec, axis=0)).astype(BF)
                gl_s[d, n] = jnp.broadcast_to(jnp.concatenate(gl, axis=1), (8, wp))
                x = jnp.where(lv2 == 0, 1.0, 0.0) - jnp.where(lv2 == 1, lowp, 0.0)
                items.append(dict(d=d, n=n, r0=r0, lowp=lowp, rhs=rhs, x=x))

        for k in range(2, n_lev + 1):
            xb = [it["x"].astype(BF) for it in items]
            t1 = []
            for it, b in zip(items, xb):
                e = jnp.where(lv2 == k, it["lowp"], 0.0).astype(BF)
                t1.append(_dot(b, bdiag(e[:, :c], e[:, c:])).astype(BF))
            t2 = [_dot(t, bdiag(b[:, :c], b[:, c:])) for t, b in zip(t1, xb)]
            for it, t in zip(items, t2):
                it["x"] = it["x"] - t

        sols = [_dot(it["x"].astype(BF), bdiag(it["rhs"][0], it["rhs"][1])) for it in items]
        for it, sol in zip(items, sols):
            d, n, r0 = it["d"], it["n"], it["r0"]
            u_s[d, pl.ds(r0, c), :] = jnp.concatenate([sol[:, 0:DN_D], sol[:, 2 * DN_D:3 * DN_D]], axis=1)
            wq_s[d, n, 0:c, :] = jnp.concatenate([sol[:, DN_D:2 * DN_D], sol[:, 3 * DN_D:]], axis=1).astype(BF)
        return carry

    lax.fori_loop(0, nc // DN_UNROLL, prep, 0)

    st_s[...] = jnp.zeros((2, DN_D, wp), F32)

    def scan(i, carry):
        ns = (i, nc - 1 - i)
        sts = [st_s[d] for d in range(2)]
        prs = []
        for d in range(2):
            sb = sts[d].astype(BF)
            prs.append(_dot(wq_s[d, ns[d]], bdiag(sb[:, :DN_D], sb[:, DN_D:])))
        r2s = []
        for d in range(2):
            r0 = pl.multiple_of(ns[d] * c, c)
            vn = (u_s[d, pl.ds(r0, c), :] - prs[d][:c]).astype(BF)
            r2s.append(_dot(atk_s[d, ns[d]], bdiag(vn[:, :DN_D], vn[:, DN_D:])))
        for d in range(2):
            r0 = pl.multiple_of(ns[d] * c, c)
            os_s[d, pl.ds(r0, c), :] = prs[d][c:] + r2s[d][:c]
            st_s[d] = sts[d] * gl_s[d, ns[d]][0:1, :] + r2s[d][c:]
        return carry

    lax.fori_loop(0, nc, scan, 0)

    o = os_s[0] + os_s[1]
    sil = _silu(z_ref[...].astype(F32))
    for hh in range(DN_HP):
        l = slice(hh * DN_D, (hh + 1) * DN_D)
        o_ref[:, l] = (_rms(o[:, l], onw_ref[0]) * sil[:, l]).astype(BF)


def _deltanet(l, qkv, z, ba, gpar, onw):
    c, nc, wp = DN_C, DN_NC, DN_HP * DN_D
    npair = DN_H // DN_HP
    col = lambda off: pl.BlockSpec((S, wp), lambda b, p: (b, off + p))
    return pl.pallas_call(
        _dn_kernel,
        grid=(B, npair),
        in_specs=[col(0), col(npair), col(2 * npair), col(0),
                  pl.BlockSpec((S, 128), lambda b, p: (b, p)),
                  pl.BlockSpec((1, 1, 2, 128), lambda b, p: (l, p, 0, 0)), _layer_spec(l, (1, DN_D))],
        out_specs=col(0),
        out_shape=jax.ShapeDtypeStruct((T, DN_W), BF),
        scratch_shapes=[pltpu.VMEM((S, 128), F32),
                        pltpu.VMEM((c, c), jnp.int32), pltpu.VMEM((c, c), jnp.int32),
                        pltpu.VMEM((2, S, wp), F32),
                        pltpu.VMEM((2, nc, 2 * c, wp), BF), pltpu.VMEM((2, nc, 2 * c, wp), BF),
                        pltpu.VMEM((2, nc, 8, wp), F32),
                        pltpu.VMEM((2, S, wp), F32), pltpu.VMEM((2, DN_D, wp), F32)],
        compiler_params=_cparams(("parallel", "parallel")),
        name="deltanet",
    )(qkv, qkv, qkv, z, ba, gpar, onw)


TM_MG = 512


def _merge_kernel(oa_ref, ob_ref, oc_ref, gate_ref, bg_ref, wbr_ref, wo_ref, x_ref, nw1_ref, gt1_ref,
                  nw2_ref, sc2_ref, sh2_ref, x1_ref, h2_ref):
    branches = (oa_ref, ob_ref, oc_ref)
    halves = [slice(k * TM_MG // 2, (k + 1) * TM_MG // 2) for k in range(2)]

    def ups(r):
        return [_dot(o_ref[r, :], wbr_ref[0, n]) for n, o_ref in enumerate(branches)]

    def gated(r, up):
        acc = None
        for n in range(3):
            g = _sigmoid(gate_ref[r, n * D:(n + 1) * D].astype(F32) + bg_ref[0, n:n + 1, :])
            acc = g * up[n] if acc is None else acc + g * up[n]
        return acc.astype(BF)

    def finish(r, y):
        x1 = x_ref[r, :] + gt1_ref[0] * _rms(y, nw1_ref[0])
        x1_ref[r, :] = x1
        h2_ref[r, :] = (_rms(x1, nw2_ref[0]) * (1.0 + sc2_ref[0]) + sh2_ref[0]).astype(BF)

    up0 = ups(halves[0])
    up1 = ups(halves[1])
    y0 = _dot(gated(halves[0], up0), wo_ref[0])
    y1 = _dot(gated(halves[1], up1), wo_ref[0])
    finish(halves[0], y0)
    finish(halves[1], y1)


def _merge(l, oa, ob, oc, gate, bg, wbr, wo, x, mod3, nw):
    tm = TM_MG
    per_b = S // tm
    row = lambda n: pl.BlockSpec((tm, n), lambda i: (i, 0))
    modspec = lambda k: pl.BlockSpec((1, 1, D), lambda i: (l * B * 6 + (i // per_b) * 6 + k, 0, 0))
    return pl.pallas_call(
        _merge_kernel,
        grid=(T // tm,),
        in_specs=[row(512), row(512), row(512), row(3 * D), _layer_spec(l, (3, D)),
                  _layer_spec(l, (3, 512, D)), _layer_spec(l, (D, D)), row(D), _norm_spec(l, 1),
                  modspec(2), _norm_spec(l, 2), modspec(4), modspec(3)],
        out_specs=[row(D), row(D)],
        out_shape=[jax.ShapeDtypeStruct((T, D), F32), jax.ShapeDtypeStruct((T, D), BF)],
        compiler_params=_cparams(("parallel",)),
        name="merge",
    )(oa, ob, oc, gate, bg, wbr, wo, x, nw, mod3, nw, mod3, mod3)


TM_FF = 512
TF = 256
FF_HALO = 16
FF_DOWN_TILES = 6


def _ffn_kernel(h_ref, x1_ref, wup_ref, cw_ref, cb_ref, wdn_ref, nw_ref, gt_ref, o_ref, act_s):
    i = pl.program_id(1)
    nblk = S // TM_FF
    r0 = pl.multiple_of(i * TM_FF, TM_FF)
    top0 = pl.multiple_of(jnp.maximum(r0 - FF_HALO, 0), FF_HALO)
    bot0 = pl.multiple_of(jnp.minimum(r0 + TM_FF, S - FF_HALO), FF_HALO)
    zero = jnp.zeros((FF_HALO, D), BF)
    top = jnp.where(i > 0, h_ref[pl.ds(top0, FF_HALO), :], zero)
    bot = jnp.where(i < nblk - 1, h_ref[pl.ds(bot0, FF_HALO), :], zero)
    slab = jnp.concatenate([top, h_ref[pl.ds(r0, TM_FF), :], bot], axis=0)

    def up(col):
        return _dot(slab, wup_ref[0, :, col:col + TF])

    def conv(p, col):
        cw = cw_ref[0, :, col:col + TF]
        h0 = FF_HALO
        rows = TM_FF + 2 * FF_HALO
        prev = pltpu.roll(p, shift=1, axis=0)[h0:h0 + TM_FF]
        nxt_ = pltpu.roll(p, shift=rows - 1, axis=0)[h0:h0 + TM_FF]
        return (prev * cw[0:1] + p[h0:h0 + TM_FF] * cw[1:2] + nxt_ * cw[2:3] + cb_ref[0, :, col:col + TF])

    nf = D_FF // TF
    nxt = (up(0), up(D_FF))
    y = None
    for j in range(nf):
        pa, pb = nxt
        if j + 1 < nf:
            nxt = (up((j + 1) * TF), up(D_FF + (j + 1) * TF))
        a = conv(pa, j * TF)
        b = conv(pb, D_FF + j * TF)
        act_s[:, j * TF:(j + 1) * TF] = (_silu(a) * b).astype(BF)
        if (j + 1) % FF_DOWN_TILES == 0 or j + 1 == nf:
            k0 = (j // FF_DOWN_TILES) * FF_DOWN_TILES * TF
            part = _dot(act_s[:, k0:(j + 1) * TF], wdn_ref[0, k0:(j + 1) * TF, :])
            y = part if y is None else y + part
    o_ref[...] = x1_ref[...] + gt_ref[0] * _rms(y, nw_ref[0])


def _ffn(l, h2, x1, w_up, conv_w, conv_b, w_down, mod3, nw):
    nblk = S // TM_FF
    row = pl.BlockSpec((TM_FF, D), lambda b, i: (b * nblk + i, 0))
    return pl.pallas_call(
        _ffn_kernel,
        grid=(B, nblk),
        in_specs=[pl.BlockSpec((S, D), lambda b, i: (b, 0)), row,
                  _layer_spec(l, (D, 2 * D_FF)), _layer_spec(l, (3, 2 * D_FF)), _layer_spec(l, (1, 2 * D_FF)),
                  _layer_spec(l, (D_FF, D)), _norm_spec(l, 3),
                  pl.BlockSpec((1, 1, D), lambda b, i: (l * B * 6 + b * 6 + 5, 0, 0))],
        out_specs=row,
        out_shape=jax.ShapeDtypeStruct((T, D), F32),
        scratch_shapes=[pltpu.VMEM((TM_FF, D_FF), BF)],
        compiler_params=_cparams(("parallel", "arbitrary")),
        name="conv_ffn",
    )(h2, x1, w_up, conv_w, conv_b, w_down, nw, mod3)


def _pair_lanes(t):
    t = t.reshape(t.shape[:-1] + (2, DN_H // DN_HP, DN_HP))
    t = jnp.moveaxis(t, -2, -3)
    return t.reshape(t.shape[:-2] + (2 * DN_HP,))


def _pair_rows(t):
    t = t.reshape(DEPTH, 2, DN_H // DN_HP, DN_HP, D)
    return jnp.moveaxis(t, 2, 1).reshape(DEPTH, DN_H // DN_HP, 2 * DN_HP, D)


def _prep_w_in(w_in):
    wt = jnp.swapaxes(w_in, 1, 2)
    a_uv, qkv, z, beta, alpha, cq, ckv, kr, gate = jnp.split(
        wt, [1024, 2560, 3072, 3080, 3088, 3472, 3728, 3792], axis=1)
    kr_sw = jnp.concatenate([kr[:, 32:], kr[:, :32]], axis=1)
    pad = jnp.zeros((DEPTH, DN_H // DN_HP, 120, D), w_in.dtype)
    ba = jnp.concatenate([_pair_rows(beta), _pair_rows(alpha), pad], axis=2).reshape(DEPTH, 256, D)
    return jnp.concatenate([a_uv, qkv, z, cq, ckv, kr, kr_sw, ba, gate], axis=1).astype(BF)


def _prep_gpar(a_log, dt_bias):
    def lanes(t):
        tp = _pair_lanes(t.reshape(DEPTH, 2 * DN_H))
        z4 = jnp.zeros((DEPTH, DN_H // DN_HP, 4), F32)
        z120 = jnp.zeros((DEPTH, DN_H // DN_HP, 120), F32)
        return jnp.concatenate([z4, tp, z120], axis=-1)
    return jnp.stack([lanes(a_log), lanes(dt_bias)], axis=2)


def _prep_w_uq(w_uq):
    wq = w_uq.reshape(DEPTH, ML_QR, ML_H, ML_DQ)
    nope = wq[..., :ML_NOPE].reshape(DEPTH, ML_QR, ML_H * ML_NOPE)
    pe = wq[..., ML_NOPE:]
    pe_sw = jnp.concatenate([pe[..., 32:], pe[..., :32]], axis=-1)
    return jnp.concatenate([nope, pe.reshape(DEPTH, ML_QR, 256), pe_sw.reshape(DEPTH, ML_QR, 256)],
                           axis=-1).astype(BF)


def _prep_w_ukv(w_ukv):
    wk = w_ukv.reshape(DEPTH, ML_KVR, ML_H, ML_NOPE + ML_V)
    return jnp.concatenate([wk[..., :ML_NOPE].reshape(DEPTH, ML_KVR, 512),
                            wk[..., ML_NOPE:].reshape(DEPTH, ML_KVR, 512)], axis=-1).astype(BF)


@jax.jit
def _forward(x, c, positions, w_ada, b_ada, norm_w, w_in, b_gate, a_ln_w, a_ln_b, a_w_sp, a_b_sp, dn_conv_w,
             dn_a_log, dn_dt_bias, dn_o_norm_w, mla_q_norm_w, mla_kv_norm_w, mla_w_uq, mla_w_ukv, w_branch,
             w_o, ffn_w_up, ffn_conv_w, ffn_conv_b, ffn_w_down):
    mod = _modulation(c, w_ada, b_ada)
    cc, ss = _rope_tables(positions)
    w_in_r = _prep_w_in(w_in)
    wuq_r = _prep_w_uq(mla_w_uq)
    wukv_r = _prep_w_ukv(mla_w_ukv)
    wsp = a_w_sp.astype(BF)
    bsp = jnp.broadcast_to(a_b_sp[..., None], (DEPTH, 4, 128, 128))
    wbr = w_branch.astype(BF)
    wo = w_o.astype(BF)
    wup = ffn_w_up.astype(BF)
    wdn = ffn_w_down.astype(BF)
    gpar = _prep_gpar(dn_a_log, dn_dt_bias)
    mod3 = mod.reshape(DEPTH * B * 6, 1, D)
    lnw = a_ln_w.reshape(DEPTH, 1, A_W)
    lnb = a_ln_b.reshape(DEPTH, 1, A_W)
    qnw = mla_q_norm_w.reshape(DEPTH, 1, ML_QR)
    kvnw = mla_kv_norm_w.reshape(DEPTH, 1, ML_KVR)
    onw = dn_o_norm_w.reshape(DEPTH, 1, DN_D)
    cb = ffn_conv_b.reshape(DEPTH, 1, 2 * D_FF)
    nw = norm_w.reshape(DEPTH * 4, 1, D)

    xf = x.reshape(T, D)
    for l in range(DEPTH):
        oa, qkv, z, ba, gate, q, k, v = _in_proj(l, xf, mod3, nw, w_in_r, lnw, lnb, wsp, bsp, qnw, kvnw,
                                                 wuq_r, wukv_r, cc, ss, dn_conv_w)
        ob = _deltanet(l, qkv, z, ba, gpar, onw)
        oc = _attention(q, k, v)
        x1, h2 = _merge(l, oa, ob, oc, gate, b_gate, wbr, wo, xf, mod3, nw)
        xf = _ffn(l, h2, x1, wup, ffn_conv_w, cb, wdn, mod3, nw)
    return xf.reshape(B, S, D)


def kernel(x, c, positions, w_ada, b_ada, norm_w, w_in, b_gate, a_ln_w, a_ln_b, a_w_sp, a_b_sp, dn_conv_w,
           dn_a_log, dn_dt_bias, dn_o_norm_w, mla_q_norm_w, mla_kv_norm_w, mla_w_uq, mla_w_ukv, w_branch, w_o,
           ffn_w_up, ffn_conv_w, ffn_conv_b, ffn_w_down):
    return _forward(x, c, positions, w_ada, b_ada, norm_w, w_in, b_gate, a_ln_w, a_ln_b, a_w_sp, a_b_sp,
                    dn_conv_w, dn_a_log, dn_dt_bias, dn_o_norm_w, mla_q_norm_w, mla_kv_norm_w, mla_w_uq,
                    mla_w_ukv, w_branch, w_o, ffn_w_up, ffn_conv_w, ffn_conv_b, ffn_w_down)
```
